```python
import jax
import jax.numpy as jnp
from jax import lax
import numpy as np


D_MODEL = 1024
BATCH = 4
SEQ = 4096
DEPTH = 1

CHUNK = 64
Q_BLOCK = 128
HEAD_DIM = 64
N_HEADS_A = 8
N_HEADS_B = 8
N_IDX_HEADS = 8
IDX_DIM = 64
MAX_SELECTED_KEYS = 256
ROT_DIM = HEAD_DIM // 4
ROPE_THETA = 500000.0
N_EXPERTS = 32
TOP_K = 4
D_EXPERT = D_MODEL
SWIGLU_LIMIT = 7.0
SWIGLU_ALPHA = 1.702
EXPERT_BLOCK = 256
RMS_EPS = 1e-6
PROJ_WIDTHS = (N_HEADS_A * HEAD_DIM, HEAD_DIM, HEAD_DIM, N_IDX_HEADS * IDX_DIM, IDX_DIM, N_IDX_HEADS, N_HEADS_B * HEAD_DIM, N_HEADS_B * HEAD_DIM, N_HEADS_B * HEAD_DIM, D_MODEL, D_MODEL)
D_PROJ = sum(PROJ_WIDTHS)

kernel_name = 'hybrid_dsa_stickbreak_moe_block'


def split_points():
    pts, acc = [], 0
    for w in PROJ_WIDTHS[:-1]:
        acc += w
        pts.append(acc)
    return pts


def rms_norm(x, g):
    xf = x.astype(jnp.float32)
    y = xf * lax.rsqrt(jnp.mean(xf * xf, axis=-1, keepdims=True) + RMS_EPS)
    return (y * g.astype(jnp.float32)).astype(x.dtype)


def rope_tables(seq):
    pos = jnp.arange(seq, dtype=jnp.float32)
    inv_freq = ROPE_THETA ** (-jnp.arange(0, ROT_DIM, 2, dtype=jnp.float32) / ROT_DIM)
    ang = pos[:, None] * inv_freq[None, :]
    return jnp.cos(ang)[:, None, :], jnp.sin(ang)[:, None, :]


def partial_rope(x, cos, sin):
    half = ROT_DIM // 2
    x1 = x[..., :half].astype(jnp.float32)
    x2 = x[..., half:ROT_DIM].astype(jnp.float32)
    rot = jnp.concatenate([x1 * cos - x2 * sin, x2 * cos + x1 * sin], axis=-1).astype(x.dtype)
    return jnp.concatenate([rot, x[..., ROT_DIM:]], axis=-1)


def to_blocks(a):
    b, s = a.shape[:2]
    return jnp.moveaxis(a.reshape(b, s // Q_BLOCK, Q_BLOCK, *a.shape[2:]), 1, 0)


def from_blocks(a):
    nb, b, qb = a.shape[:3]
    return jnp.moveaxis(a, 0, 1).reshape(b, nb * qb, -1)


def dsa_attention(q, k, v, q_idx, k_idx, w_idx):
    s_len = q.shape[1]
    n_sel = min(MAX_SELECTED_KEYS, s_len // 4)
    key_pos = jnp.arange(s_len)

    def block(args):
        qb, qib, wb, bi = args
        t = bi * Q_BLOCK + jnp.arange(Q_BLOCK)
        limit = (t // CHUNK + 1) * CHUNK
        admissible = key_pos[None, :] < limit[:, None]
        rel = jax.nn.relu(jnp.einsum('bqhd,bsd->bqhs', qib, k_idx) * (IDX_DIM ** -0.5))
        score = jnp.einsum('bqhs,bqh->bqs', rel, wb).astype(jnp.float32)
        score = jnp.where(admissible[None], score, -jnp.inf)
        _, sel = lax.top_k(score, n_sel)
        valid = sel < limit[None, :, None]
        k_sel = jax.vmap(lambda kb, ib: kb[ib])(k, sel)
        v_sel = jax.vmap(lambda vb, ib: vb[ib])(v, sel)
        logits = jnp.einsum('bqhd,bqnd->bqhn', qb, k_sel).astype(jnp.float32) * (HEAD_DIM ** -0.5)
        logits = jnp.where(valid[:, :, None, :], logits, -jnp.inf)
        p = jax.nn.softmax(logits, axis=-1).astype(v.dtype)
        return jnp.einsum('bqhn,bqnd->bqhd', p, v_sel)

    nb = s_len // Q_BLOCK
    out = lax.map(block, (to_blocks(q), to_blocks(q_idx), to_blocks(w_idx), jnp.arange(nb)))
    return from_blocks(out)


def stick_breaking_attention(q, k, v):
    s_len = q.shape[1]
    key_pos = jnp.arange(s_len)

    def block(args):
        qb, bi = args
        t = bi * Q_BLOCK + jnp.arange(Q_BLOCK)
        causal = key_pos[None, :] < t[:, None]
        z = jnp.einsum('bqhd,bshd->bhqs', qb, k).astype(jnp.float32) * (HEAD_DIM ** -0.5)
        log_beta = jax.nn.log_sigmoid(z)
        log_keep = jnp.where(causal, jax.nn.log_sigmoid(-z), 0.0)
        later = lax.cumsum(log_keep, axis=3, reverse=True) - log_keep
        a = jnp.where(causal, jnp.exp(log_beta + later), 0.0).astype(v.dtype)
        return jnp.einsum('bhqs,bshd->bqhd', a, v)

    nb = s_len // Q_BLOCK
    out = lax.map(block, (to_blocks(q), jnp.arange(nb)))
    return from_blocks(out)


def hybrid_mixer(xn, w_in, q_norm_g, k_norm_g, w_proj_a, w_proj_b, w_out, cos, sin):
    b, s, _ = xn.shape
    proj = xn @ w_in
    qa, ka, va, qi, ki, wi, qb, kb, vb, ga, gb = jnp.split(proj, split_points(), axis=-1)
    qa = partial_rope(rms_norm(qa.reshape(b, s, N_HEADS_A, HEAD_DIM), q_norm_g), cos, sin)
    ka = partial_rope(rms_norm(ka.reshape(b, s, 1, HEAD_DIM), k_norm_g), cos, sin)[:, :, 0]
    qi = partial_rope(qi.reshape(b, s, N_IDX_HEADS, IDX_DIM), cos, sin)
    ki = partial_rope(ki.reshape(b, s, 1, IDX_DIM), cos, sin)[:, :, 0]
    wi = wi * (N_IDX_HEADS ** -0.5)
    oa = dsa_attention(qa, ka, va, qi, ki, wi)
    ob = stick_breaking_attention(qb.reshape(b, s, N_HEADS_B, HEAD_DIM), kb.reshape(b, s, N_HEADS_B, HEAD_DIM), vb.reshape(b, s, N_HEADS_B, HEAD_DIM))
    merged = jax.nn.sigmoid(ga) * (oa @ w_proj_a) + jax.nn.sigmoid(gb) * (ob @ w_proj_b)
    return merged @ w_out


def moe_ffn(xf, w_router, b_router, w_gu, b_gu, w_down, b_down):
    n, d = xf.shape
    logits = (xf @ w_router + b_router).astype(jnp.float32)
    top_vals, top_idx = lax.top_k(logits, TOP_K)
    gates = jax.nn.softmax(top_vals, axis=-1)
    m = n * TOP_K
    e_flat = top_idx.reshape(m)
    tok_flat = jnp.repeat(jnp.arange(n, dtype=jnp.int32), TOP_K)
    g_flat = gates.reshape(m)
    order = jnp.argsort(e_flat)
    e_sorted = e_flat[order]
    counts = jnp.bincount(e_flat, length=N_EXPERTS)
    padded = ((counts + EXPERT_BLOCK - 1) // EXPERT_BLOCK) * EXPERT_BLOCK
    start = jnp.cumsum(counts) - counts
    pend = jnp.cumsum(padded)
    pstart = pend - padded
    dest = pstart[e_sorted] + (jnp.arange(m) - start[e_sorted])
    n_blocks = (m + N_EXPERTS * (EXPERT_BLOCK - 1) + EXPERT_BLOCK - 1) // EXPERT_BLOCK
    rows = n_blocks * EXPERT_BLOCK
    tok_buf = jnp.zeros((rows,), jnp.int32).at[dest].set(tok_flat[order])
    gate_buf = jnp.zeros((rows,), xf.dtype).at[dest].set(g_flat[order].astype(xf.dtype))
    block_start = jnp.arange(n_blocks) * EXPERT_BLOCK
    block_expert = jnp.minimum(jnp.sum(pend[None, :] <= block_start[:, None], axis=1), N_EXPERTS - 1)
    x_blocks = xf[tok_buf].reshape(n_blocks, EXPERT_BLOCK, d)

    def expert_block(args):
        xb, e = args
        hgu = xb @ w_gu[e] + b_gu[e]
        gate = jnp.minimum(hgu[:, :D_EXPERT], SWIGLU_LIMIT)
        up = jnp.clip(hgu[:, D_EXPERT:], -SWIGLU_LIMIT, SWIGLU_LIMIT)
        act = (up + 1.0) * gate * jax.nn.sigmoid(SWIGLU_ALPHA * gate)
        return act @ w_down[e] + b_down[e]

    y_blocks = lax.map(expert_block, (x_blocks, block_expert)).reshape(rows, d)
    return jnp.zeros_like(xf).at[tok_buf].add(gate_buf[:, None] * y_blocks)


def setup_inputs(seed: int = 0) -> dict:
    key = jax.random.key(seed)
    ks = jax.random.split(key, 16)

    def normal(k, shape, scale):
        return jax.random.normal(k, shape, jnp.float32) * scale

    L = DEPTH
    wa = N_HEADS_A * HEAD_DIM
    wb = N_HEADS_B * HEAD_DIM
    return {
        'x': normal(ks[0], (BATCH, SEQ, D_MODEL), 1.0),
        'norm_mix_g': 1.0 + normal(ks[1], (L, D_MODEL), 0.02),
        'w_in': normal(ks[2], (L, D_MODEL, D_PROJ), D_MODEL ** -0.5),
        'q_norm_g': 1.0 + normal(ks[3], (L, HEAD_DIM), 0.02),
        'k_norm_g': 1.0 + normal(ks[4], (L, HEAD_DIM), 0.02),
        'w_proj_a': normal(ks[5], (L, wa, D_MODEL), wa ** -0.5),
        'w_proj_b': normal(ks[6], (L, wb, D_MODEL), wb ** -0.5),
        'w_out': normal(ks[7], (L, D_MODEL, D_MODEL), D_MODEL ** -0.5),
        'norm_ffn_g': 1.0 + normal(ks[8], (L, D_MODEL), 0.02),
        'w_router': normal(ks[9], (L, D_MODEL, N_EXPERTS), D_MODEL ** -0.5),
        'b_router': normal(ks[10], (L, N_EXPERTS), 0.01),
        'w_gate_up': normal(ks[11], (L, N_EXPERTS, D_MODEL, 2 * D_EXPERT), D_MODEL ** -0.5),
        'b_gate_up': normal(ks[12], (L, N_EXPERTS, 2 * D_EXPERT), 0.01),
        'w_down': normal(ks[13], (L, N_EXPERTS, D_EXPERT, D_MODEL), D_EXPERT ** -0.5),
        'b_down': normal(ks[14], (L, N_EXPERTS, D_MODEL), 0.01),
    }


def reference(x, norm_mix_g, w_in, q_norm_g, k_norm_g, w_proj_a, w_proj_b, w_out, norm_ffn_g, w_router, b_router, w_gate_up, b_gate_up, w_down, b_down):
    b, s, d = x.shape
    cos, sin = rope_tables(s)
    h = x
    for l in range(DEPTH):
        h = h + hybrid_mixer(rms_norm(h, norm_mix_g[l]), w_in[l], q_norm_g[l], k_norm_g[l], w_proj_a[l], w_proj_b[l], w_out[l], cos, sin)
        hn = rms_norm(h, norm_ffn_g[l]).reshape(b * s, d)
        h = h + moe_ffn(hn, w_router[l], b_router[l], w_gate_up[l], b_gate_up[l], w_down[l], b_down[l]).reshape(b, s, d)
    return h
```

```python
import functools

import jax
import jax.numpy as jnp
from jax import lax
from jax.experimental import pallas as pl
from jax.experimental.pallas import tpu as pltpu

F32 = jnp.float32
BF16 = jnp.bfloat16
I32 = jnp.int32

D_MODEL = 1024
CHUNK = 64
HEAD_DIM = 64
N_HEADS = 8
MAX_SELECTED_KEYS = 256
ROT_DIM = HEAD_DIM // 4
ROPE_THETA = 500000.0
N_EXPERTS = 32
TOP_K = 4
D_EXPERT = D_MODEL
SWIGLU_LIMIT = 7.0
SWIGLU_ALPHA = 1.702
EXPERT_BLOCK = 256
RMS_EPS = 1e-6

LANES = 128
VMEM_LIMIT = 56 * 1024 * 1024

ROW_TILE = 256
DSA_QB = 128
DSA_KC = 512
SB_T = 256
COMBINE_T = 128
DISPATCH_T = 64

NEG_BIG = -1e30


def _nt_dot(a, b):
    return lax.dot_general(a, b, (((1,), (1,)), ((), ())), preferred_element_type=F32)


def _split_bf16(x):
    hi = x.astype(BF16)
    lo = (x - hi.astype(F32)).astype(BF16)
    return hi, lo


_OFF_QA, _OFF_QI, _OFF_QB, _OFF_KB, _OFF_VB = 0, 512, 1024, 1536, 2048
_OFF_KK, _OFF_VW, _OFF_GA, _OFF_GB, _W_COLS = 2560, 2688, 2816, 3840, 4864


def _rope128(y, c, s1, s2):
    return y * c + pltpu.roll(y, 8, 1) * s1 + pltpu.roll(y, LANES - 8, 1) * s2


def _head_rms_inv(x, norm_lo, norm_hi):
    lane = lax.broadcasted_iota(I32, x.shape, 1)
    lo = lane < HEAD_DIM
    sq = x * x
    one = jnp.ones((x.shape[0], 1), F32)
    inv_lo = one
    inv_hi = one
    if norm_lo:
        s = jnp.sum(jnp.where(lo, sq, 0.0), axis=-1, keepdims=True)
        inv_lo = lax.rsqrt(s * (1.0 / HEAD_DIM) + RMS_EPS)
    if norm_hi:
        s = jnp.sum(jnp.where(lo, 0.0, sq), axis=-1, keepdims=True)
        inv_hi = lax.rsqrt(s * (1.0 / HEAD_DIM) + RMS_EPS)
    return jnp.where(lo, inv_lo, inv_hi)


def _inproj_kernel(x_ref, gmix_ref, w_ref, qg_ref, kg_ref, c_ref, s1_ref, s2_ref,
                   qa_ref, qi_ref, qb_ref, kb_ref, vb_ref, ka_ref, ki_ref, va_ref,
                   wi_ref, ga_ref, gb_ref):
    x = x_ref[...]
    ms = jnp.mean(x * x, axis=-1, keepdims=True)
    xn = (x * lax.rsqrt(ms + RMS_EPS) * gmix_ref[...]).astype(BF16)

    def proj(off, width):
        return jnp.dot(xn, w_ref[:, off:off + width], preferred_element_type=F32)

    c, s1, s2 = c_ref[...], s1_ref[...], s2_ref[...]
    scale = HEAD_DIM ** -0.5

    def put_heads(dst_ref, pair, val):
        dst_ref[2 * pair] = val[:, :HEAD_DIM].astype(BF16)
        dst_ref[2 * pair + 1] = val[:, HEAD_DIM:].astype(BF16)

    acc = proj(_OFF_QA, 512)
    for p in range(4):
        blk = acc[:, p * LANES:(p + 1) * LANES]
        y = blk * _head_rms_inv(blk, True, True) * qg_ref[...]
        put_heads(qa_ref, p, _rope128(y, c, s1, s2) * scale)

    acc = proj(_OFF_QI, 512)
    for p in range(4):
        blk = acc[:, p * LANES:(p + 1) * LANES]
        put_heads(qi_ref, p, _rope128(blk, c, s1, s2) * scale)

    acc = proj(_OFF_QB, 512)
    for p in range(4):
        put_heads(qb_ref, p, acc[:, p * LANES:(p + 1) * LANES] * scale)
    acc = proj(_OFF_KB, 512)
    for p in range(4):
        put_heads(kb_ref, p, acc[:, p * LANES:(p + 1) * LANES])
    acc = proj(_OFF_VB, 512)
    for p in range(4):
        put_heads(vb_ref, p, acc[:, p * LANES:(p + 1) * LANES])

    kk = proj(_OFF_KK, LANES)
    y = kk * _head_rms_inv(kk, True, False) * kg_ref[...]
    y = _rope128(y, c, s1, s2)
    ka_ref[...] = y[:, :HEAD_DIM].astype(BF16)
    ki_ref[...] = y[:, HEAD_DIM:].astype(BF16)

    vw = proj(_OFF_VW, LANES)
    va_ref[...] = vw[:, :HEAD_DIM].astype(BF16)
    wi_ref[...] = vw[:, HEAD_DIM:HEAD_DIM + N_HEADS]

    ga_ref[...] = proj(_OFF_GA, D_MODEL)
    gb_ref[...] = proj(_OFF_GB, D_MODEL)


def _rope_tables(seq):
    pos = jnp.arange(seq, dtype=F32)
    inv_freq = ROPE_THETA ** (-jnp.arange(0, ROT_DIM, 2, dtype=F32) / ROT_DIM)
    ang = pos[:, None] * inv_freq[None, :]
    cos, sin = jnp.cos(ang), jnp.sin(ang)
    half = ROT_DIM // 2
    zeros = jnp.zeros((seq, HEAD_DIM - ROT_DIM), F32)
    zh = jnp.zeros((seq, half), F32)
    c64 = jnp.concatenate([cos, cos, zeros + 1.0], axis=1)
    s1_64 = jnp.concatenate([zh, sin, zeros], axis=1)
    s2_64 = jnp.concatenate([-sin, zh, zeros], axis=1)
    tile2 = lambda t: jnp.concatenate([t, t], axis=1)
    return tile2(c64), tile2(s1_64), tile2(s2_64)


def _inproj(x2, norm_g, w_in, q_norm_g, k_norm_g, seq):
    n = x2.shape[0]
    w = w_in
    sl = lambda a, b: w[:, a:b]
    o = [0, 512, 576, 640, 1152, 1216, 1224, 1736, 2248, 2760, 3784, 4808]
    pad = jnp.zeros((D_MODEL, LANES - HEAD_DIM - N_HEADS), w.dtype)
    w_all = jnp.concatenate([
        sl(o[0], o[1]), sl(o[3], o[4]), sl(o[6], o[7]), sl(o[7], o[8]), sl(o[8], o[9]),
        sl(o[1], o[2]), sl(o[4], o[5]),
        sl(o[2], o[3]), sl(o[5], o[6]), pad,
        sl(o[9], o[10]), sl(o[10], o[11])], axis=1).astype(BF16)
    assert w_all.shape[1] == _W_COLS
    qg = jnp.concatenate([q_norm_g, q_norm_g])[None, :]
    kg = jnp.concatenate([k_norm_g, jnp.ones_like(k_norm_g)])[None, :]
    c, s1, s2 = _rope_tables(seq)
    tm = ROW_TILE
    pos_blocks = seq // tm
    full = lambda shape: pl.BlockSpec(shape, lambda i: (0,) * len(shape))
    head_out = jax.ShapeDtypeStruct((N_HEADS, n, HEAD_DIM), BF16)
    head_spec = pl.BlockSpec((N_HEADS, tm, HEAD_DIM), lambda i: (0, i, 0))
    row64 = jax.ShapeDtypeStruct((n, HEAD_DIM), BF16)
    row64_spec = pl.BlockSpec((tm, HEAD_DIM), lambda i: (i, 0))
    tab_spec = pl.BlockSpec((tm, LANES), lambda i: (i % pos_blocks, 0))
    return pl.pallas_call(
        _inproj_kernel,
        grid=(n // tm,),
        in_specs=[pl.BlockSpec((tm, D_MODEL), lambda i: (i, 0)),
                  full((1, D_MODEL)), full((D_MODEL, _W_COLS)),
                  full((1, LANES)), full((1, LANES)), tab_spec, tab_spec, tab_spec],
        out_specs=[head_spec] * 5 + [row64_spec] * 3
                  + [pl.BlockSpec((tm, N_HEADS), lambda i: (i, 0)),
                     pl.BlockSpec((tm, D_MODEL), lambda i: (i, 0)),
                     pl.BlockSpec((tm, D_MODEL), lambda i: (i, 0))],
        out_shape=[head_out] * 5 + [row64] * 3
                  + [jax.ShapeDtypeStruct((n, N_HEADS), F32),
                     jax.ShapeDtypeStruct((n, D_MODEL), F32),
                     jax.ShapeDtypeStruct((n, D_MODEL), F32)],
        compiler_params=pltpu.CompilerParams(vmem_limit_bytes=VMEM_LIMIT),
        name="inproj",
    )(x2, norm_g[None, :], w_all, qg, kg, c, s1, s2)


def _key_to_float(u):
    k = u ^ jnp.int32(-2 ** 31)
    b = k ^ ((k >> 31) & jnp.int32(0x7FFFFFFF))
    return lax.bitcast_convert_type(b, F32)


def _lane_fold(m):
    acc = m[:, :LANES]
    for j in range(1, m.shape[1] // LANES):
        acc = acc + m[:, j * LANES:(j + 1) * LANES]
    return acc


def _dsa_kernel(n_sel, qa_ref, qi_ref, wi_ref, ka_ref, ki_ref, va_ref, o_ref, score_ref):
    i = pl.program_id(1)
    qb, kc_w = DSA_QB, DSA_KC
    n_chunks = (i * qb + qb + kc_w - 1) // kc_w
    t_pos = i * qb + lax.broadcasted_iota(I32, (qb, 1), 0)
    limit = (t_pos // CHUNK + 1) * CHUNK
    kiota = lax.broadcasted_iota(I32, (1, kc_w), 1)

    q_idx = qi_ref[...].reshape(N_HEADS * qb, HEAD_DIM)
    w = wi_ref[...] * (N_HEADS ** -0.5)

    def idx_body(c, carry):
        start = pl.multiple_of(c * kc_w, kc_w)
        r = _nt_dot(q_idx, ki_ref[pl.ds(start, kc_w), :])
        r = jnp.maximum(r, 0.0).reshape(N_HEADS, qb, kc_w)
        sc = r[0] * w[:, 0:1]
        for h in range(1, N_HEADS):
            sc = sc + r[h] * w[:, h:h + 1]
        sc = jnp.where(start + kiota < limit, sc, -jnp.inf)
        score_ref[c] = sc
        return carry

    lax.fori_loop(0, n_chunks, idx_body, 0)

    def count(pred_fn):
        def body(c, acc):
            m = pred_fn(score_ref[c], c * kc_w + kiota)
            return acc + _lane_fold(m.astype(F32))
        acc = lax.fori_loop(0, n_chunks, body, jnp.zeros((qb, LANES), F32))
        return jnp.sum(acc, axis=-1, keepdims=True)

    def thr_body(it, u):
        trial = u | (jnp.int32(1) << (31 - it))
        cand = _key_to_float(trial)
        cnt = count(lambda s, kp: s >= cand)
        return jnp.where(cnt >= n_sel, trial, u)

    u = lax.fori_loop(0, 32, thr_body, jnp.zeros((qb, 1), I32))
    select_all = limit <= n_sel
    thr = jnp.where(select_all, -jnp.inf, _key_to_float(u))
    n_gt = count(lambda s, kp: s > thr)
    need = n_sel - n_gt

    def tie_body(it, j):
        trial = j | (jnp.int32(1) << (12 - it))
        g = count(lambda s, kp: (s == thr) & (kp < trial))
        return jnp.where(g <= need, trial, j)

    j_lim = lax.fori_loop(0, 13, tie_body, jnp.zeros((qb, 1), I32))
    j_lim = jnp.where(select_all, jnp.int32(2 ** 30), j_lim)

    q = qa_ref[...].reshape(N_HEADS * qb, HEAD_DIM)

    def att_body(c, carry):
        m, l, acc = carry
        start = pl.multiple_of(c * kc_w, kc_w)
        kpos = start + kiota
        sc = score_ref[c]
        sel = (kpos < limit) & ((sc > thr) | ((sc == thr) & (kpos < j_lim)))
        sel3 = sel[None, :, :]
        s = _nt_dot(q, ka_ref[pl.ds(start, kc_w), :]).reshape(N_HEADS, qb, kc_w)
        s = jnp.where(sel3, s, NEG_BIG)
        m_new = jnp.maximum(m, jnp.max(s, axis=-1, keepdims=True))
        alpha = jnp.exp(m - m_new)
        p = jnp.where(sel3, jnp.exp(s - m_new), 0.0)
        l = alpha * l + jnp.sum(p, axis=-1, keepdims=True)
        pv = jnp.dot(p.reshape(N_HEADS * qb, kc_w).astype(BF16), va_ref[pl.ds(start, kc_w), :],
                     preferred_element_type=F32)
        acc = acc * alpha + pv.reshape(N_HEADS, qb, HEAD_DIM)
        return m_new, l, acc

    m0 = jnp.full((N_HEADS, qb, 1), NEG_BIG, F32)
    l0 = jnp.zeros((N_HEADS, qb, 1), F32)
    a0 = jnp.zeros((N_HEADS, qb, HEAD_DIM), F32)
    _, l, acc = lax.fori_loop(0, n_chunks, att_body, (m0, l0, a0))
    o_ref[...] = (acc / l).astype(BF16)


def _dsa(qa, qi, wi, ka, ki, va, batch, seq):
    n = batch * seq
    nqb = seq // DSA_QB
    n_sel = min(MAX_SELECTED_KEYS, seq // 4)
    head_spec = pl.BlockSpec((N_HEADS, DSA_QB, HEAD_DIM), lambda b, i: (0, b * nqb + i, 0))
    kv_spec = pl.BlockSpec((seq, HEAD_DIM), lambda b, i: (b, 0))
    return pl.pallas_call(
        functools.partial(_dsa_kernel, n_sel),
        grid=(batch, nqb),
        in_specs=[head_spec, head_spec,
                  pl.BlockSpec((DSA_QB, N_HEADS), lambda b, i: (b * nqb + i, 0)),
                  kv_spec, kv_spec, kv_spec],
        out_specs=head_spec,
        out_shape=jax.ShapeDtypeStruct((N_HEADS, n, HEAD_DIM), BF16),
        scratch_shapes=[pltpu.VMEM((seq // DSA_KC, DSA_QB, DSA_KC), F32)],
        compiler_params=pltpu.CompilerParams(vmem_limit_bytes=VMEM_LIMIT),
        name="dsa",
    )(qa, qi, wi, ka, ki, va)


def _sb_kernel(q_ref, k_ref, v_ref, o_ref):
    i = pl.program_id(2)
    t = SB_T
    q = q_ref[0]
    row = lax.broadcasted_iota(I32, (t, t), 0)
    col = lax.broadcasted_iota(I32, (t, t), 1)
    tri = jnp.where(row > col, 1.0, 0.0).astype(BF16)
    causal = col < row

    def block(j, carry, acc, diag):
        start = pl.multiple_of(j * t, t)
        z = _nt_dot(q, k_ref[0, pl.ds(start, t), :])
        lk = -(jnp.maximum(z, 0.0) + jnp.log(1.0 + jnp.exp(-jnp.abs(z))))
        if diag:
            lk = jnp.where(causal, lk, 0.0)
        hi, lo = _split_bf16(lk)
        later = (jnp.dot(hi, tri, preferred_element_type=F32)
                 + jnp.dot(lo, tri, preferred_element_type=F32))
        a = jnp.exp(z + lk + later + carry)
        if diag:
            a = jnp.where(causal, a, 0.0)
        acc = acc + jnp.dot(a.astype(BF16), v_ref[0, pl.ds(start, t), :],
                            preferred_element_type=F32)
        carry = carry + jnp.sum(lk, axis=-1, keepdims=True)
        return carry, acc

    carry, acc = block(i, jnp.zeros((t, 1), F32), jnp.zeros((t, HEAD_DIM), F32), True)
    carry, acc = lax.fori_loop(0, i, lambda jj, c: block(i - 1 - jj, c[0], c[1], False),
                               (carry, acc))
    o_ref[0] = acc.astype(BF16)


def _sb(qb, kb, vb, batch, seq):
    n = batch * seq
    nq = seq // SB_T
    q_spec = pl.BlockSpec((1, SB_T, HEAD_DIM), lambda b, h, i: (h, b * nq + i, 0))
    kv_spec = pl.BlockSpec((1, seq, HEAD_DIM), lambda b, h, i: (h, b, 0))
    return pl.pallas_call(
        _sb_kernel,
        grid=(batch, N_HEADS, nq),
        in_specs=[q_spec, kv_spec, kv_spec],
        out_specs=q_spec,
        out_shape=jax.ShapeDtypeStruct((N_HEADS, n, HEAD_DIM), BF16),
        compiler_params=pltpu.CompilerParams(vmem_limit_bytes=VMEM_LIMIT),
        name="stickbreak",
    )(qb, kb, vb)


def _outproj_kernel(x_ref, oa_ref, ob_ref, ga_ref, gb_ref, wpa_ref, wpb_ref, wo_ref,
                    gffn_ref, wr_hi_ref, wr_lo_ref, br_ref,
                    h_ref, hn_ref, idx_ref, gate_ref, rank_ref, cnt_ref, carry_ref):
    step = pl.program_id(0)
    tm = x_ref.shape[0]

    @pl.when(step == 0)
    def _():
        carry_ref[...] = jnp.zeros_like(carry_ref)

    pa = jnp.dot(oa_ref[0], wpa_ref[0], preferred_element_type=F32)
    pb = jnp.dot(ob_ref[0], wpb_ref[0], preferred_element_type=F32)
    for hd in range(1, N_HEADS):
        pa = pa + jnp.dot(oa_ref[hd], wpa_ref[hd], preferred_element_type=F32)
        pb = pb + jnp.dot(ob_ref[hd], wpb_ref[hd], preferred_element_type=F32)
    merged = jax.nn.sigmoid(ga_ref[...]) * pa + jax.nn.sigmoid(gb_ref[...]) * pb
    h = x_ref[...] + jnp.dot(merged.astype(BF16), wo_ref[...], preferred_element_type=F32)
    h_ref[...] = h
    ms = jnp.mean(h * h, axis=-1, keepdims=True)
    hn = h * lax.rsqrt(ms + RMS_EPS) * gffn_ref[...]
    hn_ref[...] = hn

    hn_hi, hn_lo = _split_bf16(hn)
    logits = (jnp.dot(hn_hi, wr_hi_ref[...], preferred_element_type=F32)
              + jnp.dot(hn_lo, wr_hi_ref[...], preferred_element_type=F32)
              + jnp.dot(hn_hi, wr_lo_ref[...], preferred_element_type=F32)) + br_ref[...]

    lane = lax.broadcasted_iota(I32, (tm, N_EXPERTS), 1)
    lane_k = lax.broadcasted_iota(I32, (tm, TOP_K), 1)
    vals = logits
    picks, top_vals, top_idx = [], [], []
    for _ in range(TOP_K):
        mx = jnp.max(vals, axis=-1, keepdims=True)
        first = jnp.min(jnp.where(vals == mx, lane, N_EXPERTS), axis=-1, keepdims=True)
        pick = lane == first
        picks.append(pick)
        top_vals.append(mx)
        top_idx.append(first)
        vals = jnp.where(pick, -jnp.inf, vals)
    exps = [jnp.exp(v - top_vals[0]) for v in top_vals]
    denom = exps[0] + exps[1] + exps[2] + exps[3]

    any_pick = picks[0] | picks[1] | picks[2] | picks[3]
    onehot = jnp.where(any_pick, 1.0, 0.0)
    r_i = lax.broadcasted_iota(I32, (tm, tm), 0)
    c_i = lax.broadcasted_iota(I32, (tm, tm), 1)
    lower = jnp.where(c_i < r_i, 1.0, 0.0).astype(BF16)
    pos = jnp.dot(lower, onehot.astype(BF16), preferred_element_type=F32) + carry_ref[...]
    carry_ref[...] = carry_ref[...] + jnp.sum(onehot, axis=0, keepdims=True)
    cnt_ref[...] = carry_ref[...].astype(I32)

    idx4 = jnp.zeros((tm, TOP_K), I32)
    gate4 = jnp.zeros((tm, TOP_K), F32)
    rank4 = jnp.zeros((tm, TOP_K), I32)
    for k in range(TOP_K):
        rk = jnp.sum(jnp.where(picks[k], pos, 0.0), axis=-1, keepdims=True).astype(I32)
        idx4 = jnp.where(lane_k == k, top_idx[k], idx4)
        gate4 = jnp.where(lane_k == k, exps[k] / denom, gate4)
        rank4 = jnp.where(lane_k == k, rk, rank4)
    idx_ref[...] = idx4
    gate_ref[...] = gate4
    rank_ref[...] = rank4


def _outproj(x2, oa, ob, ga, gb, w_proj_a, w_proj_b, w_out, norm_ffn_g, w_router, b_router):
    n = x2.shape[0]
    tm = ROW_TILE
    wpa = w_proj_a.astype(BF16).reshape(N_HEADS, HEAD_DIM, D_MODEL)
    wpb = w_proj_b.astype(BF16).reshape(N_HEADS, HEAD_DIM, D_MODEL)
    wr_hi = w_router.astype(BF16)
    wr_lo = (w_router - wr_hi.astype(F32)).astype(BF16)
    full = lambda shape: pl.BlockSpec(shape, lambda i: (0,) * len(shape))
    row = lambda width: pl.BlockSpec((tm, width), lambda i: (i, 0))
    head_spec = pl.BlockSpec((N_HEADS, tm, HEAD_DIM), lambda i: (0, i, 0))
    return pl.pallas_call(
        _outproj_kernel,
        grid=(n // tm,),
        in_specs=[row(D_MODEL), head_spec, head_spec, row(D_MODEL), row(D_MODEL),
                  full((N_HEADS, HEAD_DIM, D_MODEL)), full((N_HEADS, HEAD_DIM, D_MODEL)),
                  full((D_MODEL, D_MODEL)), full((1, D_MODEL)),
                  full((D_MODEL, N_EXPERTS)), full((D_MODEL, N_EXPERTS)), full((1, N_EXPERTS))],
        out_specs=[row(D_MODEL), row(D_MODEL), row(TOP_K), row(TOP_K), row(TOP_K),
                   full((1, N_EXPERTS))],
        out_shape=[jax.ShapeDtypeStruct((n, D_MODEL), F32), jax.ShapeDtypeStruct((n, D_MODEL), F32),
                   jax.ShapeDtypeStruct((n, TOP_K), I32), jax.ShapeDtypeStruct((n, TOP_K), F32),
                   jax.ShapeDtypeStruct((n, TOP_K), I32), jax.ShapeDtypeStruct((1, N_EXPERTS), I32)],
        scratch_shapes=[pltpu.VMEM((1, N_EXPERTS), F32)],
        compiler_params=pltpu.CompilerParams(vmem_limit_bytes=VMEM_LIMIT,
                                             dimension_semantics=("arbitrary",)),
        name="outproj_router",
    )(x2, oa, ob, ga, gb, wpa, wpb, w_out.astype(BF16), norm_ffn_g[None, :],
      wr_hi, wr_lo, b_router[None, :])


def _dispatch_kernel(dest_ref, fill_lo_ref, fill_hi_ref, hn_ref, xs_ref, zero_ref, sem, zsem):
    n = hn_ref.shape[0]
    ct = DISPATCH_T
    n_chunks = n // ct
    zero_ref[...] = jnp.zeros_like(zero_ref)

    def wait_rows(rows):
        pltpu.make_async_copy(hn_ref.at[pl.ds(0, rows)], xs_ref.at[pl.ds(0, rows)], sem).wait()

    def chunk(c, carry):
        def tok(tt, carry2):
            t = c * ct + tt
            for k in range(TOP_K):
                d = dest_ref[t * TOP_K + k]
                pltpu.make_async_copy(hn_ref.at[pl.ds(t, 1)], xs_ref.at[pl.ds(d, 1)], sem).start()
            return carry2
        lax.fori_loop(0, ct, tok, 0)

        @pl.when(c >= 2)
        def _():
            wait_rows(ct * TOP_K)
        return carry

    lax.fori_loop(0, n_chunks, chunk, 0)
    for _ in range(min(2, n_chunks)):
        wait_rows(ct * TOP_K)

    def expert(e, total):
        lo, hi = fill_lo_ref[e], fill_hi_ref[e]

        def fill(r, carry):
            pltpu.make_async_copy(zero_ref, xs_ref.at[pl.ds(r, 1)], zsem).start()
            return carry
        lax.fori_loop(lo, hi, fill, 0)
        return total + (hi - lo)

    total = lax.fori_loop(0, N_EXPERTS, expert, jnp.int32(0))

    def drain(r, carry):
        pltpu.make_async_copy(zero_ref, xs_ref.at[pl.ds(0, 1)], zsem).wait()
        return carry
    lax.fori_loop(0, total, drain, 0)


def _dispatch(dest_flat, fill_lo, fill_hi, hn, rows):
    n = hn.shape[0]
    return pl.pallas_call(
        _dispatch_kernel,
        grid_spec=pltpu.PrefetchScalarGridSpec(
            num_scalar_prefetch=3,
            grid=(1,),
            in_specs=[pl.BlockSpec(memory_space=pl.ANY)],
            out_specs=pl.BlockSpec(memory_space=pl.ANY),
            scratch_shapes=[pltpu.VMEM((1, D_MODEL), F32),
                            pltpu.SemaphoreType.DMA(()), pltpu.SemaphoreType.DMA(())]),
        out_shape=jax.ShapeDtypeStruct((rows, D_MODEL), F32),
        compiler_params=pltpu.CompilerParams(has_side_effects=True),
        name="dispatch",
    )(dest_flat, fill_lo, fill_hi, hn)


def _expert_kernel(be_ref, nu_ref, x_ref, wgu_ref, bgu_ref, wd_ref, bd_ref, y_ref,
                   wgu_bf, wd_bf):
    b = pl.program_id(0)
    n_used = nu_ref[0]
    prev = be_ref[jnp.maximum(b - 1, 0)]

    @pl.when((b < n_used) & ((b == 0) | (be_ref[b] != prev)))
    def _():
        wgu_bf[...] = wgu_ref[0].astype(BF16)
        wd_bf[...] = wd_ref[0].astype(BF16)

    @pl.when(b < n_used)
    def _():
        xb = x_ref[...].astype(BF16)
        hgu = jnp.dot(xb, wgu_bf[...], preferred_element_type=F32) + bgu_ref[0]
        gate = jnp.minimum(hgu[:, :D_EXPERT], SWIGLU_LIMIT)
        up = jnp.clip(hgu[:, D_EXPERT:], -SWIGLU_LIMIT, SWIGLU_LIMIT)
        act = (up + 1.0) * gate * jax.nn.sigmoid(SWIGLU_ALPHA * gate)
        y_ref[...] = (jnp.dot(act.astype(BF16), wd_bf[...], preferred_element_type=F32)
                      + bd_ref[0])


def _experts(block_expert, n_used, xs, w_gu, b_gu, w_down, b_down):
    rows = xs.shape[0]
    n_blocks = rows // EXPERT_BLOCK

    def blk(b, be, nu):
        return jnp.minimum(b, nu[0] - 1)

    row_spec = pl.BlockSpec((EXPERT_BLOCK, D_MODEL), lambda b, be, nu: (blk(b, be, nu), 0))
    w_spec = lambda d1, d2: pl.BlockSpec((1, d1, d2), lambda b, be, nu: (be[blk(b, be, nu)], 0, 0))
    return pl.pallas_call(
        _expert_kernel,
        grid_spec=pltpu.PrefetchScalarGridSpec(
            num_scalar_prefetch=2,
            grid=(n_blocks,),
            in_specs=[row_spec, w_spec(D_MODEL, 2 * D_EXPERT), w_spec(1, 2 * D_EXPERT),
                      w_spec(D_EXPERT, D_MODEL), w_spec(1, D_MODEL)],
            out_specs=row_spec,
            scratch_shapes=[pltpu.VMEM((D_MODEL, 2 * D_EXPERT), BF16),
                            pltpu.VMEM((D_EXPERT, D_MODEL), BF16)]),
        out_shape=jax.ShapeDtypeStruct((rows, D_MODEL), F32),
        compiler_params=pltpu.CompilerParams(vmem_limit_bytes=VMEM_LIMIT,
                                             dimension_semantics=("arbitrary",)),
        name="experts",
    )(block_expert, n_used, xs, w_gu, b_gu[:, None, :], w_down, b_down[:, None, :])


def _combine_kernel(dest_ref, h_ref, gate_ref, y_ref, o_ref, buf_ref, sems):
    s = pl.program_id(0)
    n_steps = pl.num_programs(0)
    ct = COMBINE_T

    def issue(step, slot):
        def tok(tt, carry):
            t = step * ct + tt
            for k in range(TOP_K):
                d = dest_ref[t * TOP_K + k]
                pltpu.make_async_copy(y_ref.at[pl.ds(d, 1)], buf_ref.at[slot, k, pl.ds(tt, 1)],
                                      sems.at[slot]).start()
            return carry
        lax.fori_loop(0, ct, tok, 0)

    @pl.when(s == 0)
    def _():
        issue(0, 0)

    @pl.when(s + 1 < n_steps)
    def _():
        issue(s + 1, (s + 1) % 2)

    slot = s % 2
    for k in range(TOP_K):
        pltpu.make_async_copy(y_ref.at[pl.ds(0, ct)], buf_ref.at[slot, k], sems.at[slot]).wait()
    g = gate_ref[...]
    out = h_ref[...]
    for k in range(TOP_K):
        out = out + g[:, k:k + 1] * buf_ref[slot, k]
    o_ref[...] = out


def _combine(dest_flat, h, gates, y):
    n = h.shape[0]
    ct = COMBINE_T
    return pl.pallas_call(
        _combine_kernel,
        grid_spec=pltpu.PrefetchScalarGridSpec(
            num_scalar_prefetch=1,
            grid=(n // ct,),
            in_specs=[pl.BlockSpec((ct, D_MODEL), lambda i, d: (i, 0)),
                      pl.BlockSpec((ct, TOP_K), lambda i, d: (i, 0)),
                      pl.BlockSpec(memory_space=pl.ANY)],
            out_specs=pl.BlockSpec((ct, D_MODEL), lambda i, d: (i, 0)),
            scratch_shapes=[pltpu.VMEM((2, TOP_K, ct, D_MODEL), F32),
                            pltpu.SemaphoreType.DMA((2,))]),
        out_shape=jax.ShapeDtypeStruct((n, D_MODEL), F32),
        compiler_params=pltpu.CompilerParams(vmem_limit_bytes=VMEM_LIMIT,
                                             dimension_semantics=("arbitrary",)),
        name="combine",
    )(dest_flat, h, gates, y)


def _moe(h, hn, top_idx, gates, rank, counts, w_gate_up, b_gate_up, w_down, b_down):
    n = h.shape[0]
    m = n * TOP_K
    n_blocks = (m + N_EXPERTS * (EXPERT_BLOCK - 1) + EXPERT_BLOCK - 1) // EXPERT_BLOCK
    rows = n_blocks * EXPERT_BLOCK
    counts = counts[0]
    padded = ((counts + EXPERT_BLOCK - 1) // EXPERT_BLOCK) * EXPERT_BLOCK
    pend = jnp.cumsum(padded).astype(I32)
    pstart = pend - padded
    dest = (pstart[top_idx] + rank).reshape(m).astype(I32)
    block_start = jnp.arange(n_blocks, dtype=I32) * EXPERT_BLOCK
    block_expert = jnp.minimum(jnp.sum(pend[None, :] <= block_start[:, None], axis=1),
                               N_EXPERTS - 1).astype(I32)
    n_used = (pend[-1:] // EXPERT_BLOCK).astype(I32)
    xs = _dispatch(dest, (pstart + counts).astype(I32), pend, hn, rows)
    y = _experts(block_expert, n_used, xs, w_gate_up, b_gate_up, w_down, b_down)
    return _combine(dest, h, gates, y)


def kernel(x, norm_mix_g, w_in, q_norm_g, k_norm_g, w_proj_a, w_proj_b, w_out, norm_ffn_g,
           w_router, b_router, w_gate_up, b_gate_up, w_down, b_down):
    batch, seq, d = x.shape
    h = x.reshape(batch * seq, d)
    for l in range(norm_mix_g.shape[0]):
        (qa, qi, qb, kb, vb, ka, ki, va, wi, ga, gb) = _inproj(
            h, norm_mix_g[l], w_in[l], q_norm_g[l], k_norm_g[l], seq)
        oa = _dsa(qa, qi, wi, ka, ki, va, batch, seq)
        ob = _sb(qb, kb, vb, batch, seq)
        h_mid, hn, top_idx, gates, rank, counts = _outproj(
            h, oa, ob, ga, gb, w_proj_a[l], w_proj_b[l], w_out[l], norm_ffn_g[l],
            w_router[l], b_router[l])
        h = _moe(h_mid, hn, top_idx, gates, rank, counts,
                 w_gate_up[l], b_gate_up[l], w_down[l], b_down[l])
    return h.reshape(batch, seq, d)
```

```python
import functools

import jax
import jax.numpy as jnp
from jax import lax
from jax.experimental import pallas as pl
from jax.experimental.pallas import tpu as pltpu

F32 = jnp.float32
BF16 = jnp.bfloat16
I32 = jnp.int32

D_MODEL = 1024
CHUNK = 64
HEAD_DIM = 64
N_HEADS = 8
MAX_SELECTED_KEYS = 256
ROT_DIM = HEAD_DIM // 4
ROPE_THETA = 500000.0
N_EXPERTS = 32
TOP_K = 4
D_EXPERT = D_MODEL
SWIGLU_LIMIT = 7.0
SWIGLU_ALPHA = 1.702
EXPERT_BLOCK = 256
RMS_EPS = 1e-6

LANES = 128
VMEM_LIMIT = 56 * 1024 * 1024

ROW_TILE = 256
DSA_QB = 128
DSA_KC = 512
SB_T = 256
SB_HEADS = 4
COMBINE_T = 128
INVERT_STEPS = 32

NEG_BIG = -1e30


def _nt_dot(a, b):
    return lax.dot_general(a, b, (((1,), (1,)), ((), ())), preferred_element_type=F32)


def _split_bf16(x):
    hi = x.astype(BF16)
    lo = (x - hi.astype(F32)).astype(BF16)
    return hi, lo


_OFF_QA, _OFF_QI, _OFF_QB, _OFF_KB, _OFF_VB = 0, 512, 1024, 1536, 2048
_OFF_KK, _OFF_VW, _OFF_GA, _OFF_GB, _W_COLS = 2560, 2688, 2816, 3840, 4864


def _rope128(y, c, s1, s2):
    return y * c + pltpu.roll(y, 8, 1) * s1 + pltpu.roll(y, LANES - 8, 1) * s2


def _head_rms_inv(x, norm_lo, norm_hi):
    lane = lax.broadcasted_iota(I32, x.shape, 1)
    lo = lane < HEAD_DIM
    sq = x * x
    one = jnp.ones((x.shape[0], 1), F32)
    inv_lo = one
    inv_hi = one
    if norm_lo:
        s = jnp.sum(jnp.where(lo, sq, 0.0), axis=-1, keepdims=True)
        inv_lo = lax.rsqrt(s * (1.0 / HEAD_DIM) + RMS_EPS)
    if norm_hi:
        s = jnp.sum(jnp.where(lo, 0.0, sq), axis=-1, keepdims=True)
        inv_hi = lax.rsqrt(s * (1.0 / HEAD_DIM) + RMS_EPS)
    return jnp.where(lo, inv_lo, inv_hi)


def _inproj_kernel(x_ref, gmix_ref, w_ref, qg_ref, kg_ref, c_ref, s1_ref, s2_ref,
                   qa_ref, qi_ref, qb_ref, kb_ref, vb_ref, ka_ref, ki_ref, va_ref,
                   wi_ref, ga_ref, gb_ref):
    x = x_ref[...]
    ms = jnp.mean(x * x, axis=-1, keepdims=True)
    xn = (x * lax.rsqrt(ms + RMS_EPS) * gmix_ref[...]).astype(BF16)

    def proj(off, width):
        return jnp.dot(xn, w_ref[:, off:off + width], preferred_element_type=F32)

    c, s1, s2 = c_ref[...], s1_ref[...], s2_ref[...]
    scale = HEAD_DIM ** -0.5

    def put_heads(dst_ref, pair, val):
        dst_ref[2 * pair] = val[:, :HEAD_DIM].astype(BF16)
        dst_ref[2 * pair + 1] = val[:, HEAD_DIM:].astype(BF16)

    acc = proj(_OFF_QA, 512)
    for p in range(4):
        blk = acc[:, p * LANES:(p + 1) * LANES]
        y = blk * _head_rms_inv(blk, True, True) * qg_ref[...]
        put_heads(qa_ref, p, _rope128(y, c, s1, s2) * scale)

    acc = proj(_OFF_QI, 512)
    for p in range(4):
        blk = acc[:, p * LANES:(p + 1) * LANES]
        put_heads(qi_ref, p, _rope128(blk, c, s1, s2) * scale)

    acc = proj(_OFF_QB, 512)
    for p in range(4):
        put_heads(qb_ref, p, acc[:, p * LANES:(p + 1) * LANES] * scale)
    acc = proj(_OFF_KB, 512)
    for p in range(4):
        put_heads(kb_ref, p, acc[:, p * LANES:(p + 1) * LANES])
    acc = proj(_OFF_VB, 512)
    for p in range(4):
        put_heads(vb_ref, p, acc[:, p * LANES:(p + 1) * LANES])

    kk = proj(_OFF_KK, LANES)
    y = kk * _head_rms_inv(kk, True, False) * kg_ref[...]
    y = _rope128(y, c, s1, s2)
    ka_ref[...] = y[:, :HEAD_DIM].astype(BF16)
    ki_ref[...] = y[:, HEAD_DIM:].astype(BF16)

    vw = proj(_OFF_VW, LANES)
    va_ref[...] = vw[:, :HEAD_DIM].astype(BF16)
    wi_ref[...] = vw[:, HEAD_DIM:HEAD_DIM + N_HEADS]

    ga_ref[...] = proj(_OFF_GA, D_MODEL)
    gb_ref[...] = proj(_OFF_GB, D_MODEL)


def _rope_tables(seq):
    pos = jnp.arange(seq, dtype=F32)
    inv_freq = ROPE_THETA ** (-jnp.arange(0, ROT_DIM, 2, dtype=F32) / ROT_DIM)
    ang = pos[:, None] * inv_freq[None, :]
    cos, sin = jnp.cos(ang), jnp.sin(ang)
    half = ROT_DIM // 2
    zeros = jnp.zeros((seq, HEAD_DIM - ROT_DIM), F32)
    zh = jnp.zeros((seq, half), F32)
    c64 = jnp.concatenate([cos, cos, zeros + 1.0], axis=1)
    s1_64 = jnp.concatenate([zh, sin, zeros], axis=1)
    s2_64 = jnp.concatenate([-sin, zh, zeros], axis=1)
    tile2 = lambda t: jnp.concatenate([t, t], axis=1)
    return tile2(c64), tile2(s1_64), tile2(s2_64)


def _inproj(x2, norm_g, w_in, q_norm_g, k_norm_g, seq):
    n = x2.shape[0]
    w = w_in
    sl = lambda a, b: w[:, a:b]
    o = [0, 512, 576, 640, 1152, 1216, 1224, 1736, 2248, 2760, 3784, 4808]
    pad = jnp.zeros((D_MODEL, LANES - HEAD_DIM - N_HEADS), w.dtype)
    w_all = jnp.concatenate([
        sl(o[0], o[1]), sl(o[3], o[4]), sl(o[6], o[7]), sl(o[7], o[8]), sl(o[8], o[9]),
        sl(o[1], o[2]), sl(o[4], o[5]),
        sl(o[2], o[3]), sl(o[5], o[6]), pad,
        sl(o[9], o[10]), sl(o[10], o[11])], axis=1).astype(BF16)
    assert w_all.shape[1] == _W_COLS
    qg = jnp.concatenate([q_norm_g, q_norm_g])[None, :]
    kg = jnp.concatenate([k_norm_g, jnp.ones_like(k_norm_g)])[None, :]
    c, s1, s2 = _rope_tables(seq)
    tm = ROW_TILE
    pos_blocks = seq // tm
    full = lambda shape: pl.BlockSpec(shape, lambda i: (0,) * len(shape))
    head_out = jax.ShapeDtypeStruct((N_HEADS, n, HEAD_DIM), BF16)
    head_spec = pl.BlockSpec((N_HEADS, tm, HEAD_DIM), lambda i: (0, i, 0))
    row64 = jax.ShapeDtypeStruct((n, HEAD_DIM), BF16)
    row64_spec = pl.BlockSpec((tm, HEAD_DIM), lambda i: (i, 0))
    tab_spec = pl.BlockSpec((tm, LANES), lambda i: (i % pos_blocks, 0))
    return pl.pallas_call(
        _inproj_kernel,
        grid=(n // tm,),
        in_specs=[pl.BlockSpec((tm, D_MODEL), lambda i: (i, 0)),
                  full((1, D_MODEL)), full((D_MODEL, _W_COLS)),
                  full((1, LANES)), full((1, LANES)), tab_spec, tab_spec, tab_spec],
        out_specs=[head_spec] * 5 + [row64_spec] * 3
                  + [pl.BlockSpec((tm, N_HEADS), lambda i: (i, 0)),
                     pl.BlockSpec((tm, D_MODEL), lambda i: (i, 0)),
                     pl.BlockSpec((tm, D_MODEL), lambda i: (i, 0))],
        out_shape=[head_out] * 5 + [row64] * 3
                  + [jax.ShapeDtypeStruct((n, N_HEADS), F32),
                     jax.ShapeDtypeStruct((n, D_MODEL), F32),
                     jax.ShapeDtypeStruct((n, D_MODEL), F32)],
        compiler_params=pltpu.CompilerParams(vmem_limit_bytes=VMEM_LIMIT),
        name="inproj",
    )(x2, norm_g[None, :], w_all, qg, kg, c, s1, s2)


def _key_to_float(u):
    k = u ^ jnp.int32(-2 ** 31)
    b = k ^ ((k >> 31) & jnp.int32(0x7FFFFFFF))
    return lax.bitcast_convert_type(b, F32)


def _lane_fold(m):
    acc = m[:, :LANES]
    for j in range(1, m.shape[1] // LANES):
        acc = acc + m[:, j * LANES:(j + 1) * LANES]
    return acc


def _dsa_kernel(n_sel, qa_ref, qi_ref, wi_ref, ka_ref, ki_ref, va_ref, o_ref, score_ref):
    i = pl.program_id(1)
    qb, kc_w = DSA_QB, DSA_KC
    n_chunks = (i * qb + qb + kc_w - 1) // kc_w
    t_pos = i * qb + lax.broadcasted_iota(I32, (qb, 1), 0)
    limit = (t_pos // CHUNK + 1) * CHUNK
    kiota = lax.broadcasted_iota(I32, (1, kc_w), 1)

    q_idx = qi_ref[...].reshape(N_HEADS * qb, HEAD_DIM)
    w = wi_ref[...] * (N_HEADS ** -0.5)

    def idx_body(c, carry):
        start = pl.multiple_of(c * kc_w, kc_w)
        r = _nt_dot(q_idx, ki_ref[pl.ds(start, kc_w), :])
        r = jnp.maximum(r, 0.0).reshape(N_HEADS, qb, kc_w)
        sc = r[0] * w[:, 0:1]
        for h in range(1, N_HEADS):
            sc = sc + r[h] * w[:, h:h + 1]
        sc = jnp.where(start + kiota < limit, sc, -jnp.inf)
        score_ref[c] = sc
        return carry

    lax.fori_loop(0, n_chunks, idx_body, 0)

    def count(pred_fn):
        def body(c, acc):
            m = pred_fn(score_ref[c], c * kc_w + kiota)
            return acc + _lane_fold(m.astype(F32))
        acc = lax.fori_loop(0, n_chunks, body, jnp.zeros((qb, LANES), F32))
        return jnp.sum(acc, axis=-1, keepdims=True)

    def thr_body(it, carry):
        u, n_ge = carry
        trial = u | (jnp.int32(1) << (31 - it))
        cand = _key_to_float(trial)
        cnt = count(lambda s, kp: s >= cand)
        take = cnt >= n_sel
        return jnp.where(take, trial, u), jnp.where(take, cnt, n_ge)

    u, n_ge = lax.fori_loop(0, 32, thr_body,
                            (jnp.zeros((qb, 1), I32), jnp.full((qb, 1), n_sel, F32)))
    select_all = limit <= n_sel
    thr = jnp.where(select_all, -jnp.inf, _key_to_float(u))

    def tie_search():
        need = n_sel - count(lambda s, kp: s > thr)

        def tie_body(it, j):
            trial = j | (jnp.int32(1) << (12 - it))
            g = count(lambda s, kp: (s == thr) & (kp < trial))
            return jnp.where(g <= need, trial, j)
        return lax.fori_loop(0, 13, tie_body, jnp.zeros((qb, 1), I32))

    tied = jnp.max(jnp.where((n_ge > n_sel) & jnp.logical_not(select_all), 1, 0)) > 0
    j_lim = lax.cond(tied, tie_search, lambda: jnp.full((qb, 1), 2 ** 30, I32))
    j_lim = jnp.where(select_all, jnp.int32(2 ** 30), j_lim)

    q = qa_ref[...].reshape(N_HEADS * qb, HEAD_DIM)

    def att_body(c, carry):
        m, l, acc = carry
        start = pl.multiple_of(c * kc_w, kc_w)
        kpos = start + kiota
        sc = score_ref[c]
        sel = (kpos < limit) & ((sc > thr) | ((sc == thr) & (kpos < j_lim)))
        sel3 = sel[None, :, :]
        s = _nt_dot(q, ka_ref[pl.ds(start, kc_w), :]).reshape(N_HEADS, qb, kc_w)
        s = jnp.where(sel3, s, NEG_BIG)
        m_new = jnp.maximum(m, jnp.max(s, axis=-1, keepdims=True))
        alpha = jnp.exp(m - m_new)
        p = jnp.exp(s - m_new)
        l = alpha * l + jnp.sum(p, axis=-1, keepdims=True)
        pv = jnp.dot(p.reshape(N_HEADS * qb, kc_w).astype(BF16), va_ref[pl.ds(start, kc_w), :],
                     preferred_element_type=F32)
        acc = acc * alpha + pv.reshape(N_HEADS, qb, HEAD_DIM)
        return m_new, l, acc

    m0 = jnp.full((N_HEADS, qb, 1), NEG_BIG, F32)
    l0 = jnp.zeros((N_HEADS, qb, 1), F32)
    a0 = jnp.zeros((N_HEADS, qb, HEAD_DIM), F32)
    _, l, acc = lax.fori_loop(0, n_chunks, att_body, (m0, l0, a0))
    o_ref[...] = (acc / l).astype(BF16)


def _dsa(qa, qi, wi, ka, ki, va, batch, seq):
    n = batch * seq
    nqb = seq // DSA_QB
    n_sel = min(MAX_SELECTED_KEYS, seq // 4)
    head_spec = pl.BlockSpec((N_HEADS, DSA_QB, HEAD_DIM), lambda b, i: (0, b * nqb + i, 0))
    kv_spec = pl.BlockSpec((seq, HEAD_DIM), lambda b, i: (b, 0))
    return pl.pallas_call(
        functools.partial(_dsa_kernel, n_sel),
        grid=(batch, nqb),
        in_specs=[head_spec, head_spec,
                  pl.BlockSpec((DSA_QB, N_HEADS), lambda b, i: (b * nqb + i, 0)),
                  kv_spec, kv_spec, kv_spec],
        out_specs=head_spec,
        out_shape=jax.ShapeDtypeStruct((N_HEADS, n, HEAD_DIM), BF16),
        scratch_shapes=[pltpu.VMEM((seq // DSA_KC, DSA_QB, DSA_KC), F32)],
        compiler_params=pltpu.CompilerParams(vmem_limit_bytes=VMEM_LIMIT),
        name="dsa",
    )(qa, qi, wi, ka, ki, va)


def _sb_kernel(q_ref, k_ref, v_ref, o_ref):
    i = pl.program_id(2)
    t = SB_T
    row = lax.broadcasted_iota(I32, (t, t), 0)
    col = lax.broadcasted_iota(I32, (t, t), 1)
    tri = jnp.where(row > col, 1.0, 0.0).astype(BF16)
    causal = col < row

    def block(j, state, diag):
        heads = range(SB_HEADS)
        start = pl.multiple_of(j * t, t)
        zs = [_nt_dot(q_ref[hd], k_ref[hd, pl.ds(start, t), :]) for hd in heads]
        lks = []
        for z in zs:
            lk = -(jnp.maximum(z, 0.0) + jnp.log(1.0 + jnp.exp(-jnp.abs(z))))
            lks.append(jnp.where(causal, lk, 0.0) if diag else lk)
        parts = [_split_bf16(lk) for lk in lks]
        laters = [jnp.dot(hi, tri, preferred_element_type=F32)
                  + jnp.dot(lo, tri, preferred_element_type=F32)
                  for hi, lo in parts]
        out = []
        for hd in heads:
            carry, acc = state[hd]
            a = jnp.exp(zs[hd] + lks[hd] + laters[hd] + carry)
            if diag:
                a = jnp.where(causal, a, 0.0)
            acc = acc + jnp.dot(a.astype(BF16), v_ref[hd, pl.ds(start, t), :],
                                preferred_element_type=F32)
            out.append((carry + jnp.sum(lks[hd], axis=-1, keepdims=True), acc))
        return tuple(out)

    state = tuple((jnp.zeros((t, 1), F32), jnp.zeros((t, HEAD_DIM), F32))
                  for _ in range(SB_HEADS))
    state = block(i, state, True)
    state = lax.fori_loop(0, i, lambda jj, st: block(i - 1 - jj, st, False), state)
    for hd in range(SB_HEADS):
        o_ref[hd] = state[hd][1].astype(BF16)


def _sb(qb, kb, vb, batch, seq):
    n = batch * seq
    nq = seq // SB_T
    q_spec = pl.BlockSpec((SB_HEADS, SB_T, HEAD_DIM), lambda b, h, i: (h, b * nq + i, 0))
    kv_spec = pl.BlockSpec((SB_HEADS, seq, HEAD_DIM), lambda b, h, i: (h, b, 0))
    return pl.pallas_call(
        _sb_kernel,
        grid=(batch, N_HEADS // SB_HEADS, nq),
        in_specs=[q_spec, kv_spec, kv_spec],
        out_specs=q_spec,
        out_shape=jax.ShapeDtypeStruct((N_HEADS, n, HEAD_DIM), BF16),
        compiler_params=pltpu.CompilerParams(vmem_limit_bytes=VMEM_LIMIT),
        name="stickbreak",
    )(qb, kb, vb)


def _outproj_kernel(x_ref, oa_ref, ob_ref, ga_ref, gb_ref, wpa_ref, wpb_ref, wo_ref,
                    gffn_ref, wr_hi_ref, wr_lo_ref, br_ref,
                    h_ref, hn_ref, idx_ref, gate_ref, rank_ref, cnt_ref, carry_ref):
    step = pl.program_id(0)
    tm = x_ref.shape[0]

    @pl.when(step == 0)
    def _():
        carry_ref[...] = jnp.zeros_like(carry_ref)

    pa = jnp.dot(oa_ref[0], wpa_ref[0], preferred_element_type=F32)
    pb = jnp.dot(ob_ref[0], wpb_ref[0], preferred_element_type=F32)
    for hd in range(1, N_HEADS):
        pa = pa + jnp.dot(oa_ref[hd], wpa_ref[hd], preferred_element_type=F32)
        pb = pb + jnp.dot(ob_ref[hd], wpb_ref[hd], preferred_element_type=F32)
    merged = jax.nn.sigmoid(ga_ref[...]) * pa + jax.nn.sigmoid(gb_ref[...]) * pb
    h = x_ref[...] + jnp.dot(merged.astype(BF16), wo_ref[...], preferred_element_type=F32)
    h_ref[...] = h
    ms = jnp.mean(h * h, axis=-1, keepdims=True)
    hn = h * lax.rsqrt(ms + RMS_EPS) * gffn_ref[...]
    hn_ref[...] = hn

    hn_hi, hn_lo = _split_bf16(hn)
    logits = (jnp.dot(hn_hi, wr_hi_ref[...], preferred_element_type=F32)
              + jnp.dot(hn_lo, wr_hi_ref[...], preferred_element_type=F32)
              + jnp.dot(hn_hi, wr_lo_ref[...], preferred_element_type=F32)) + br_ref[...]

    lane = lax.broadcasted_iota(I32, (tm, N_EXPERTS), 1)
    lane_k = lax.broadcasted_iota(I32, (tm, TOP_K), 1)
    vals = logits
    picks, top_vals, top_idx = [], [], []
    for _ in range(TOP_K):
        mx = jnp.max(vals, axis=-1, keepdims=True)
        first = jnp.min(jnp.where(vals == mx, lane, N_EXPERTS), axis=-1, keepdims=True)
        pick = lane == first
        picks.append(pick)
        top_vals.append(mx)
        top_idx.append(first)
        vals = jnp.where(pick, -jnp.inf, vals)
    exps = [jnp.exp(v - top_vals[0]) for v in top_vals]
    denom = exps[0] + exps[1] + exps[2] + exps[3]

    any_pick = picks[0] | picks[1] | picks[2] | picks[3]
    onehot = jnp.where(any_pick, 1.0, 0.0)
    r_i = lax.broadcasted_iota(I32, (tm, tm), 0)
    c_i = lax.broadcasted_iota(I32, (tm, tm), 1)
    lower = jnp.where(c_i < r_i, 1.0, 0.0).astype(BF16)
    pos = jnp.dot(lower, onehot.astype(BF16), preferred_element_type=F32) + carry_ref[...]
    carry_ref[...] = carry_ref[...] + jnp.sum(onehot, axis=0, keepdims=True)
    cnt_ref[...] = carry_ref[...].astype(I32)

    idx4 = jnp.zeros((tm, TOP_K), I32)
    gate4 = jnp.zeros((tm, TOP_K), F32)
    rank4 = jnp.zeros((tm, TOP_K), I32)
    for k in range(TOP_K):
        rk = jnp.sum(jnp.where(picks[k], pos, 0.0), axis=-1, keepdims=True).astype(I32)
        idx4 = jnp.where(lane_k == k, top_idx[k], idx4)
        gate4 = jnp.where(lane_k == k, exps[k] / denom, gate4)
        rank4 = jnp.where(lane_k == k, rk, rank4)
    idx_ref[...] = idx4
    gate_ref[...] = gate4
    rank_ref[...] = rank4


def _outproj(x2, oa, ob, ga, gb, w_proj_a, w_proj_b, w_out, norm_ffn_g, w_router, b_router):
    n = x2.shape[0]
    tm = ROW_TILE
    wpa = w_proj_a.astype(BF16).reshape(N_HEADS, HEAD_DIM, D_MODEL)
    wpb = w_proj_b.astype(BF16).reshape(N_HEADS, HEAD_DIM, D_MODEL)
    wr_hi = w_router.astype(BF16)
    wr_lo = (w_router - wr_hi.astype(F32)).astype(BF16)
    full = lambda shape: pl.BlockSpec(shape, lambda i: (0,) * len(shape))
    row = lambda width: pl.BlockSpec((tm, width), lambda i: (i, 0))
    head_spec = pl.BlockSpec((N_HEADS, tm, HEAD_DIM), lambda i: (0, i, 0))
    return pl.pallas_call(
        _outproj_kernel,
        grid=(n // tm,),
        in_specs=[row(D_MODEL), head_spec, head_spec, row(D_MODEL), row(D_MODEL),
                  full((N_HEADS, HEAD_DIM, D_MODEL)), full((N_HEADS, HEAD_DIM, D_MODEL)),
                  full((D_MODEL, D_MODEL)), full((1, D_MODEL)),
                  full((D_MODEL, N_EXPERTS)), full((D_MODEL, N_EXPERTS)), full((1, N_EXPERTS))],
        out_specs=[row(D_MODEL), row(D_MODEL), row(TOP_K), row(TOP_K), row(TOP_K),
                   full((1, N_EXPERTS))],
        out_shape=[jax.ShapeDtypeStruct((n, D_MODEL), F32), jax.ShapeDtypeStruct((n, D_MODEL), F32),
                   jax.ShapeDtypeStruct((n, TOP_K), I32), jax.ShapeDtypeStruct((n, TOP_K), F32),
                   jax.ShapeDtypeStruct((n, TOP_K), I32), jax.ShapeDtypeStruct((1, N_EXPERTS), I32)],
        scratch_shapes=[pltpu.VMEM((1, N_EXPERTS), F32)],
        compiler_params=pltpu.CompilerParams(vmem_limit_bytes=VMEM_LIMIT,
                                             dimension_semantics=("arbitrary",)),
        name="outproj_router",
    )(x2, oa, ob, ga, gb, wpa, wpb, w_out.astype(BF16), norm_ffn_g[None, :],
      wr_hi, wr_lo, b_router[None, :])


def _invert_kernel(dest_ref, tok_ref):
    s = pl.program_id(0)
    half = pl.num_programs(0) // 2
    rows_per = tok_ref.shape[0] // half
    m_per = dest_ref.shape[0] // half

    @pl.when(s < half)
    def _():
        def clear(r, carry):
            tok_ref[s * rows_per + r] = 0
            return carry
        lax.fori_loop(0, rows_per, clear, 0, unroll=8)

    @pl.when(s >= half)
    def _():
        def put(r, carry):
            i = (s - half) * m_per + r
            tok_ref[dest_ref[i]] = i // TOP_K
            return carry
        lax.fori_loop(0, m_per, put, 0, unroll=8)


def _invert(dest_flat, rows):
    half = INVERT_STEPS
    assert rows % half == 0 and dest_flat.shape[0] % half == 0
    return pl.pallas_call(
        _invert_kernel,
        grid_spec=pltpu.PrefetchScalarGridSpec(
            num_scalar_prefetch=1, grid=(2 * half,), in_specs=[],
            out_specs=pl.BlockSpec(memory_space=pltpu.SMEM)),
        out_shape=jax.ShapeDtypeStruct((rows,), I32),
        compiler_params=pltpu.CompilerParams(dimension_semantics=("arbitrary",)),
        name="invert",
    )(dest_flat)


def _expert_kernel(be_ref, nu_ref, tok_ref, hn_ref, wgu_ref, bgu_ref, wd_ref, bd_ref, y_ref,
                   xbuf, wgu_bf, wd_bf, sems):
    b = pl.program_id(0)
    n_used = nu_ref[0]
    eb = EXPERT_BLOCK

    def gather(blk, slot):
        def row(r, carry):
            t = tok_ref[blk * eb + r]
            pltpu.make_async_copy(hn_ref.at[pl.ds(t, 1)], xbuf.at[slot, pl.ds(r, 1)],
                                  sems.at[slot]).start()
            return carry
        lax.fori_loop(0, eb, row, 0, unroll=8)

    @pl.when(b == 0)
    def _():
        gather(0, 0)

    @pl.when(b + 1 < n_used)
    def _():
        gather(b + 1, (b + 1) % 2)

    prev = be_ref[jnp.maximum(b - 1, 0)]

    @pl.when((b < n_used) & ((b == 0) | (be_ref[b] != prev)))
    def _():
        wgu_bf[...] = wgu_ref[0].astype(BF16)
        wd_bf[...] = wd_ref[0].astype(BF16)

    @pl.when(b < n_used)
    def _():
        slot = b % 2
        pltpu.make_async_copy(hn_ref.at[pl.ds(0, eb)], xbuf.at[slot], sems.at[slot]).wait()
        xb = xbuf[slot].astype(BF16)
        hgu = jnp.dot(xb, wgu_bf[...], preferred_element_type=F32) + bgu_ref[0]
        gate = jnp.minimum(hgu[:, :D_EXPERT], SWIGLU_LIMIT)
        up = jnp.clip(hgu[:, D_EXPERT:], -SWIGLU_LIMIT, SWIGLU_LIMIT)
        act = (up + 1.0) * gate * jax.nn.sigmoid(SWIGLU_ALPHA * gate)
        y_ref[...] = (jnp.dot(act.astype(BF16), wd_bf[...], preferred_element_type=F32)
                      + bd_ref[0])

    @pl.when(b >= n_used)
    def _():
        y_ref[...] = jnp.zeros_like(y_ref)


def _experts(block_expert, n_used, tok_buf, hn, w_gu, b_gu, w_down, b_down):
    rows = tok_buf.shape[0]
    n_blocks = rows // EXPERT_BLOCK

    def blk(b, be, nu, tok):
        return jnp.minimum(b, nu[0] - 1)

    w_spec = lambda d1, d2: pl.BlockSpec(
        (1, d1, d2), lambda b, be, nu, tok: (be[blk(b, be, nu, tok)], 0, 0))
    return pl.pallas_call(
        _expert_kernel,
        grid_spec=pltpu.PrefetchScalarGridSpec(
            num_scalar_prefetch=3,
            grid=(n_blocks,),
            in_specs=[pl.BlockSpec(memory_space=pl.ANY),
                      w_spec(D_MODEL, 2 * D_EXPERT), w_spec(1, 2 * D_EXPERT),
                      w_spec(D_EXPERT, D_MODEL), w_spec(1, D_MODEL)],
            out_specs=pl.BlockSpec((EXPERT_BLOCK, D_MODEL), lambda b, be, nu, tok: (b, 0)),
            scratch_shapes=[pltpu.VMEM((2, EXPERT_BLOCK, D_MODEL), F32),
                            pltpu.VMEM((D_MODEL, 2 * D_EXPERT), BF16),
                            pltpu.VMEM((D_EXPERT, D_MODEL), BF16),
                            pltpu.SemaphoreType.DMA((2,))]),
        out_shape=jax.ShapeDtypeStruct((rows, D_MODEL), F32),
        compiler_params=pltpu.CompilerParams(vmem_limit_bytes=VMEM_LIMIT,
                                             dimension_semantics=("arbitrary",)),
        name="experts",
    )(block_expert, n_used, tok_buf, hn, w_gu, b_gu[:, None, :], w_down, b_down[:, None, :])


def _combine_kernel(dest_ref, h_ref, gate_ref, y_ref, o_ref, buf_ref, sems):
    s = pl.program_id(0)
    n_steps = pl.num_programs(0)
    ct = COMBINE_T

    def issue(step, slot):
        def tok(tt, carry):
            t = step * ct + tt
            for k in range(TOP_K):
                d = dest_ref[t * TOP_K + k]
                pltpu.make_async_copy(y_ref.at[pl.ds(d, 1)], buf_ref.at[slot, k, pl.ds(tt, 1)],
                                      sems.at[slot]).start()
            return carry
        lax.fori_loop(0, ct, tok, 0)

    @pl.when(s == 0)
    def _():
        issue(0, 0)

    @pl.when(s + 1 < n_steps)
    def _():
        issue(s + 1, (s + 1) % 2)

    slot = s % 2
    for k in range(TOP_K):
        pltpu.make_async_copy(y_ref.at[pl.ds(0, ct)], buf_ref.at[slot, k], sems.at[slot]).wait()
    g = gate_ref[...]
    out = h_ref[...]
    for k in range(TOP_K):
        out = out + g[:, k:k + 1] * buf_ref[slot, k]
    o_ref[...] = out


def _combine(dest_flat, h, gates, y):
    n = h.shape[0]
    ct = COMBINE_T
    return pl.pallas_call(
        _combine_kernel,
        grid_spec=pltpu.PrefetchScalarGridSpec(
            num_scalar_prefetch=1,
            grid=(n // ct,),
            in_specs=[pl.BlockSpec((ct, D_MODEL), lambda i, d: (i, 0)),
                      pl.BlockSpec((ct, TOP_K), lambda i, d: (i, 0)),
                      pl.BlockSpec(memory_space=pl.ANY)],
            out_specs=pl.BlockSpec((ct, D_MODEL), lambda i, d: (i, 0)),
            scratch_shapes=[pltpu.VMEM((2, TOP_K, ct, D_MODEL), F32),
                            pltpu.SemaphoreType.DMA((2,))]),
        out_shape=jax.ShapeDtypeStruct((n, D_MODEL), F32),
        compiler_params=pltpu.CompilerParams(vmem_limit_bytes=VMEM_LIMIT,
                                             dimension_semantics=("arbitrary",)),
        name="combine",
    )(dest_flat, h, gates, y)


def _moe(h, hn, top_idx, gates, rank, counts, w_gate_up, b_gate_up, w_down, b_down):
    n = h.shape[0]
    m = n * TOP_K
    n_blocks = (m + N_EXPERTS * (EXPERT_BLOCK - 1) + EXPERT_BLOCK - 1) // EXPERT_BLOCK
    rows = n_blocks * EXPERT_BLOCK
    counts = counts[0]
    padded = ((counts + EXPERT_BLOCK - 1) // EXPERT_BLOCK) * EXPERT_BLOCK
    pend = jnp.cumsum(padded).astype(I32)
    pstart = pend - padded
    dest = (pstart[top_idx] + rank).reshape(m).astype(I32)
    block_start = jnp.arange(n_blocks, dtype=I32) * EXPERT_BLOCK
    block_expert = jnp.minimum(jnp.sum(pend[None, :] <= block_start[:, None], axis=1),
                               N_EXPERTS - 1).astype(I32)
    n_used = (pend[-1:] // EXPERT_BLOCK).astype(I32)
    tok_buf = _invert(dest, rows)
    y = _experts(block_expert, n_used, tok_buf, hn, w_gate_up, b_gate_up, w_down, b_down)
    return _combine(dest, h, gates, y)


def kernel(x, norm_mix_g, w_in, q_norm_g, k_norm_g, w_proj_a, w_proj_b, w_out, norm_ffn_g,
           w_router, b_router, w_gate_up, b_gate_up, w_down, b_down):
    batch, seq, d = x.shape
    h = x.reshape(batch * seq, d)
    for l in range(norm_mix_g.shape[0]):
        (qa, qi, qb, kb, vb, ka, ki, va, wi, ga, gb) = _inproj(
            h, norm_mix_g[l], w_in[l], q_norm_g[l], k_norm_g[l], seq)
        oa = _dsa(qa, qi, wi, ka, ki, va, batch, seq)
        ob = _sb(qb, kb, vb, batch, seq)
        h_mid, hn, top_idx, gates, rank, counts = _outproj(
            h, oa, ob, ga, gb, w_proj_a[l], w_proj_b[l], w_out[l], norm_ffn_g[l],
            w_router[l], b_router[l])
        h = _moe(h_mid, hn, top_idx, gates, rank, counts,
                 w_gate_up[l], b_gate_up[l], w_down[l], b_down[l])
    return h.reshape(batch, seq, d)
```

```python
import functools

import jax
import jax.numpy as jnp
from jax import lax
from jax.experimental import pallas as pl
from jax.experimental.pallas import tpu as pltpu

F32 = jnp.float32
BF16 = jnp.bfloat16
I32 = jnp.int32

D_MODEL = 1024
CHUNK = 64
HEAD_DIM = 64
N_HEADS = 8
MAX_SELECTED_KEYS = 256
ROT_DIM = HEAD_DIM // 4
ROPE_THETA = 500000.0
N_EXPERTS = 32
TOP_K = 4
D_EXPERT = D_MODEL
SWIGLU_LIMIT = 7.0
SWIGLU_ALPHA = 1.702
EXPERT_BLOCK = 256
RMS_EPS = 1e-6

LANES = 128
VMEM_LIMIT = 56 * 1024 * 1024

ROW_TILE = 256
DSA_QB = 128
DSA_KC = 512
SB_T = 256
SB_HEADS = 8
COMBINE_T = 128
DISPATCH_T = 512

NEG_BIG = -1e30
TINY = 1e-30


def _nt_dot(a, b):
    return lax.dot_general(a, b, (((1,), (1,)), ((), ())), preferred_element_type=F32)


def _split_bf16(x):
    hi = x.astype(BF16)
    lo = (x - hi.astype(F32)).astype(BF16)
    return hi, lo


_OFF_QA, _OFF_QI, _OFF_QB, _OFF_KB, _OFF_VB = 0, 512, 1024, 1536, 2048
_OFF_KK, _OFF_VW, _OFF_GA, _OFF_GB, _W_COLS = 2560, 2688, 2816, 3840, 4864


def _rope128(y, c, s1, s2):
    return y * c + pltpu.roll(y, 8, 1) * s1 + pltpu.roll(y, LANES - 8, 1) * s2


def _head_rms_inv(x, norm_lo, norm_hi):
    lane = lax.broadcasted_iota(I32, x.shape, 1)
    lo = lane < HEAD_DIM
    sq = x * x
    one = jnp.ones((x.shape[0], 1), F32)
    inv_lo = one
    inv_hi = one
    if norm_lo:
        s = jnp.sum(jnp.where(lo, sq, 0.0), axis=-1, keepdims=True)
        inv_lo = lax.rsqrt(s * (1.0 / HEAD_DIM) + RMS_EPS)
    if norm_hi:
        s = jnp.sum(jnp.where(lo, 0.0, sq), axis=-1, keepdims=True)
        inv_hi = lax.rsqrt(s * (1.0 / HEAD_DIM) + RMS_EPS)
    return jnp.where(lo, inv_lo, inv_hi)


def _inproj_kernel(x_ref, gmix_ref, w_ref, wvt_ref, qg_ref, kg_ref, c_ref, s1_ref, s2_ref,
                   qa_ref, qi_ref, qb_ref, kb_ref, vb_ref, ka_ref, ki_ref, va_ref,
                   wi_ref, ga_ref, gb_ref):
    x = x_ref[...]
    ms = jnp.mean(x * x, axis=-1, keepdims=True)
    xn = (x * lax.rsqrt(ms + RMS_EPS) * gmix_ref[...]).astype(BF16)

    def proj(off, width):
        return jnp.dot(xn, w_ref[:, off:off + width], preferred_element_type=F32)

    c, s1, s2 = c_ref[...], s1_ref[...], s2_ref[...]
    scale = HEAD_DIM ** -0.5

    def put_heads(dst_ref, pair, val):
        dst_ref[2 * pair] = val[:, :HEAD_DIM].astype(BF16)
        dst_ref[2 * pair + 1] = val[:, HEAD_DIM:].astype(BF16)

    acc = proj(_OFF_QA, 512)
    for p in range(4):
        blk = acc[:, p * LANES:(p + 1) * LANES]
        y = blk * _head_rms_inv(blk, True, True) * qg_ref[...]
        put_heads(qa_ref, p, _rope128(y, c, s1, s2) * scale)

    acc = proj(_OFF_QI, 512)
    for p in range(4):
        blk = acc[:, p * LANES:(p + 1) * LANES]
        put_heads(qi_ref, p, _rope128(blk, c, s1, s2) * scale)

    acc = proj(_OFF_QB, 512)
    for p in range(4):
        put_heads(qb_ref, p, acc[:, p * LANES:(p + 1) * LANES] * scale)
    acc = proj(_OFF_KB, 512)
    for p in range(4):
        put_heads(kb_ref, p, acc[:, p * LANES:(p + 1) * LANES])
    acc = proj(_OFF_VB, 512)
    for p in range(4):
        put_heads(vb_ref, p, acc[:, p * LANES:(p + 1) * LANES])

    kk = proj(_OFF_KK, LANES)
    y = kk * _head_rms_inv(kk, True, False) * kg_ref[...]
    y = _rope128(y, c, s1, s2)
    ka_ref[...] = y[:, :HEAD_DIM].astype(BF16)
    ki_ref[...] = y[:, HEAD_DIM:].astype(BF16)

    vw_t = _nt_dot(wvt_ref[...], xn)
    va_ref[0] = vw_t[:HEAD_DIM].astype(BF16)
    wi_ref[...] = vw_t[HEAD_DIM:HEAD_DIM + N_HEADS]

    ga_ref[...] = proj(_OFF_GA, D_MODEL)
    gb_ref[...] = proj(_OFF_GB, D_MODEL)


def _rope_tables(seq):
    pos = jnp.arange(seq, dtype=F32)
    inv_freq = ROPE_THETA ** (-jnp.arange(0, ROT_DIM, 2, dtype=F32) / ROT_DIM)
    ang = pos[:, None] * inv_freq[None, :]
    cos, sin = jnp.cos(ang), jnp.sin(ang)
    half = ROT_DIM // 2
    zeros = jnp.zeros((seq, HEAD_DIM - ROT_DIM), F32)
    zh = jnp.zeros((seq, half), F32)
    c64 = jnp.concatenate([cos, cos, zeros + 1.0], axis=1)
    s1_64 = jnp.concatenate([zh, sin, zeros], axis=1)
    s2_64 = jnp.concatenate([-sin, zh, zeros], axis=1)
    tile2 = lambda t: jnp.concatenate([t, t], axis=1)
    return tile2(c64), tile2(s1_64), tile2(s2_64)


def _inproj(x2, norm_g, w_in, q_norm_g, k_norm_g, seq):
    n = x2.shape[0]
    w = w_in
    sl = lambda a, b: w[:, a:b]
    o = [0, 512, 576, 640, 1152, 1216, 1224, 1736, 2248, 2760, 3784, 4808]
    pad = jnp.zeros((D_MODEL, LANES - HEAD_DIM - N_HEADS), w.dtype)
    w_all = jnp.concatenate([
        sl(o[0], o[1]), sl(o[3], o[4]), sl(o[6], o[7]), sl(o[7], o[8]), sl(o[8], o[9]),
        sl(o[1], o[2]), sl(o[4], o[5]),
        sl(o[2], o[3]), sl(o[5], o[6]), pad,
        sl(o[9], o[10]), sl(o[10], o[11])], axis=1).astype(BF16)
    assert w_all.shape[1] == _W_COLS
    qg = jnp.concatenate([q_norm_g, q_norm_g])[None, :]
    kg = jnp.concatenate([k_norm_g, jnp.ones_like(k_norm_g)])[None, :]
    c, s1, s2 = _rope_tables(seq)
    tm = ROW_TILE
    pos_blocks = seq // tm
    full = lambda shape: pl.BlockSpec(shape, lambda i: (0,) * len(shape))
    head_out = jax.ShapeDtypeStruct((N_HEADS, n, HEAD_DIM), BF16)
    head_spec = pl.BlockSpec((N_HEADS, tm, HEAD_DIM), lambda i: (0, i, 0))
    row64 = jax.ShapeDtypeStruct((n, HEAD_DIM), BF16)
    row64_spec = pl.BlockSpec((tm, HEAD_DIM), lambda i: (i, 0))
    tab_spec = pl.BlockSpec((tm, LANES), lambda i: (i % pos_blocks, 0))
    return pl.pallas_call(
        _inproj_kernel,
        grid=(n // tm,),
        in_specs=[pl.BlockSpec((tm, D_MODEL), lambda i: (i, 0)),
                  full((1, D_MODEL)), full((D_MODEL, _W_COLS)), full((LANES, D_MODEL)),
                  full((1, LANES)), full((1, LANES)), tab_spec, tab_spec, tab_spec],
        out_specs=[head_spec] * 5 + [row64_spec] * 2
                  + [pl.BlockSpec((1, HEAD_DIM, tm), lambda i: (i, 0, 0)),
                     pl.BlockSpec((N_HEADS, tm), lambda i: (0, i)),
                     pl.BlockSpec((tm, D_MODEL), lambda i: (i, 0)),
                     pl.BlockSpec((tm, D_MODEL), lambda i: (i, 0))],
        out_shape=[head_out] * 5 + [row64] * 2
                  + [jax.ShapeDtypeStruct((n // tm, HEAD_DIM, tm), BF16),
                     jax.ShapeDtypeStruct((N_HEADS, n), F32),
                     jax.ShapeDtypeStruct((n, D_MODEL), F32),
                     jax.ShapeDtypeStruct((n, D_MODEL), F32)],
        compiler_params=pltpu.CompilerParams(vmem_limit_bytes=VMEM_LIMIT),
        name="inproj",
    )(x2, norm_g[None, :], w_all, w_all[:, _OFF_VW:_OFF_VW + LANES].T, qg, kg, c, s1, s2)


def _key_to_float(u):
    k = u ^ jnp.int32(-2 ** 31)
    b = k ^ ((k >> 31) & jnp.int32(0x7FFFFFFF))
    return lax.bitcast_convert_type(b, F32)


def _slab_reduce(x, op):
    parts = [x[j * 8:(j + 1) * 8] for j in range(x.shape[0] // 8)]
    while len(parts) > 1:
        nxt = [op(parts[j], parts[j + 1]) for j in range(0, len(parts) - 1, 2)]
        if len(parts) % 2:
            nxt.append(parts[-1])
        parts = nxt
    return parts[0]


def _dsa_kernel(n_sel, qa_ref, qi_ref, wi_ref, ka_ref, ki_ref, va_ref, o_ref,
                score_ref, m_ref, l_ref, acc_ref):
    i = pl.program_id(1)
    qb, kc = DSA_QB, DSA_KC
    pairs = range(N_HEADS // 2)
    va_pieces = kc // ROW_TILE
    n_chunks = (i * qb + qb + kc - 1) // kc
    t_pos = i * qb + lax.broadcasted_iota(I32, (1, qb), 1)
    limit = (t_pos // CHUNK + 1) * CHUNK
    kio = lax.broadcasted_iota(I32, (kc, qb), 0)

    def pair_rows(ref, p):
        return ref[2 * p:2 * p + 2].reshape(2 * qb, HEAD_DIM)

    def twice(x):
        return jnp.concatenate([x, x], axis=1)

    w = wi_ref[...] * (N_HEADS ** -0.5)
    w2 = [jnp.concatenate([w[2 * p:2 * p + 1], w[2 * p + 1:2 * p + 2]], axis=1) for p in pairs]
    q_idx = [pair_rows(qi_ref, p) for p in pairs]

    def idx_body(c, carry):
        start = pl.multiple_of(c * kc, kc)
        keys = ki_ref[pl.ds(start, kc), :]
        tot = None
        for p in pairs:
            t = jnp.maximum(_nt_dot(keys, q_idx[p]), 0.0) * w2[p]
            tot = t if tot is None else tot + t
        sc = tot[:, :qb] + tot[:, qb:]
        score_ref[c] = jnp.where(start + kio < limit, sc, -jnp.inf)
        return carry

    lax.fori_loop(0, n_chunks, idx_body, 0)

    def count(pred_fn):
        def body(c, acc):
            return acc + _slab_reduce(pred_fn(c).astype(F32), jnp.add)
        acc = lax.fori_loop(0, n_chunks, body, jnp.zeros((8, qb), F32))
        return jnp.sum(acc, axis=0, keepdims=True)

    def kth_largest(val_fn):
        def body(it, carry):
            u, n_ge = carry
            trial = u | (jnp.int32(1) << (31 - it))
            cand = _key_to_float(trial)
            cnt = count(lambda c: val_fn(c) >= cand)
            take = cnt >= n_sel
            return jnp.where(take, trial, u), jnp.where(take, cnt, n_ge)
        u, n_ge = lax.fori_loop(0, 32, body,
                                (jnp.zeros((1, qb), I32), jnp.full((1, qb), n_sel, F32)))
        return _key_to_float(u), n_ge

    select_all = limit <= n_sel
    thr, n_ge = kth_largest(lambda c: score_ref[c])
    thr = jnp.where(select_all, -jnp.inf, thr)

    def excess(c):
        s = score_ref[c]
        above = jnp.where(s > thr, jnp.maximum(s - thr, TINY), 0.0)
        return jnp.where(s >= thr, above, -jnp.inf)

    def refine():
        thr2, _ = kth_largest(excess)
        need = n_sel - count(lambda c: excess(c) > thr2)

        def tie_body(it, j):
            trial = j | (jnp.int32(1) << (12 - it))
            g = count(lambda c: (excess(c) == thr2) & (c * kc + kio < trial))
            return jnp.where(g <= need, trial, j)
        return thr2, lax.fori_loop(0, 13, tie_body, jnp.zeros((1, qb), I32))

    tied = jnp.max(jnp.where((n_ge > n_sel) & jnp.logical_not(select_all), 1, 0)) > 0
    thr2, j_lim = lax.cond(tied, refine, lambda: (jnp.zeros((1, qb), F32),
                                                  jnp.full((1, qb), 2 ** 30, I32)))
    thr2 = jnp.where(select_all, 0.0, thr2)
    j_lim = jnp.where(select_all, jnp.int32(2 ** 30), j_lim)

    q_att = [pair_rows(qa_ref, p) for p in pairs]
    m_ref[...] = jnp.full_like(m_ref, NEG_BIG)
    l_ref[...] = jnp.zeros_like(l_ref)
    acc_ref[...] = jnp.zeros_like(acc_ref)

    def att_body(c, carry):
        start = pl.multiple_of(c * kc, kc)
        kpos = start + kio
        ex = excess(c)
        sel = (kpos < limit) & ((ex > thr2) | ((ex == thr2) & (kpos < j_lim)))
        bias = twice(jnp.where(sel, 0.0, NEG_BIG))
        keys = ka_ref[pl.ds(start, kc), :]
        s = [_nt_dot(keys, q_att[p]) + bias for p in pairs]
        m_old = [m_ref[p] for p in pairs]
        m_new = [jnp.maximum(m_old[p], jnp.max(_slab_reduce(s[p], jnp.maximum), axis=0,
                                               keepdims=True)) for p in pairs]
        alpha = [jnp.exp(m_old[p] - m_new[p]) for p in pairs]
        pe = [jnp.exp(s[p] - m_new[p]) for p in pairs]
        pv = []
        for p in pairs:
            pb = pe[p].astype(BF16)
            tot = None
            for v in range(va_pieces):
                part = jnp.dot(va_ref[va_pieces * c + v], pb[v * ROW_TILE:(v + 1) * ROW_TILE],
                               preferred_element_type=F32)
                tot = part if tot is None else tot + part
            pv.append(tot)
        for p in pairs:
            l_ref[p] = alpha[p] * l_ref[p] + jnp.sum(_slab_reduce(pe[p], jnp.add), axis=0,
                                                     keepdims=True)
            acc_ref[p] = acc_ref[p] * alpha[p] + pv[p]
            m_ref[p] = m_new[p]
        return carry

    lax.fori_loop(0, n_chunks, att_body, 0)
    for p in pairs:
        out = acc_ref[p] / l_ref[p]
        o_ref[2 * p] = out[:, :qb].T.astype(BF16)
        o_ref[2 * p + 1] = out[:, qb:].T.astype(BF16)


def _dsa(qa, qi, wi, ka, ki, va, batch, seq):
    n = batch * seq
    nqb = seq // DSA_QB
    n_sel = min(MAX_SELECTED_KEYS, seq // 4)
    head_spec = pl.BlockSpec((N_HEADS, DSA_QB, HEAD_DIM), lambda b, i: (0, b * nqb + i, 0))
    kv_spec = pl.BlockSpec((seq, HEAD_DIM), lambda b, i: (b, 0))
    pair_stat = pltpu.VMEM((N_HEADS // 2, 1, 2 * DSA_QB), F32)
    return pl.pallas_call(
        functools.partial(_dsa_kernel, n_sel),
        grid=(batch, nqb),
        in_specs=[head_spec, head_spec,
                  pl.BlockSpec((N_HEADS, DSA_QB), lambda b, i: (0, b * nqb + i)),
                  kv_spec, kv_spec,
                  pl.BlockSpec((seq // ROW_TILE, HEAD_DIM, ROW_TILE), lambda b, i: (b, 0, 0))],
        out_specs=head_spec,
        out_shape=jax.ShapeDtypeStruct((N_HEADS, n, HEAD_DIM), BF16),
        scratch_shapes=[pltpu.VMEM((seq // DSA_KC, DSA_KC, DSA_QB), F32), pair_stat, pair_stat,
                        pltpu.VMEM((N_HEADS // 2, HEAD_DIM, 2 * DSA_QB), F32)],
        compiler_params=pltpu.CompilerParams(vmem_limit_bytes=VMEM_LIMIT),
        name="dsa",
    )(qa, qi, wi, ka, ki, va)


def _sb_kernel(q_ref, k_ref, v_ref, o_ref):
    i = pl.program_id(2)
    t = SB_T
    row = lax.broadcasted_iota(I32, (t, t), 0)
    col = lax.broadcasted_iota(I32, (t, t), 1)
    tri = jnp.where(row > col, 1.0, 0.0).astype(BF16)
    causal = col < row

    def block(j, state, diag):
        heads = range(SB_HEADS)
        start = pl.multiple_of(j * t, t)
        zs = [_nt_dot(q_ref[hd], k_ref[hd, pl.ds(start, t), :]) for hd in heads]
        lks = []
        for z in zs:
            lk = -(jnp.maximum(z, 0.0) + jnp.log(1.0 + jnp.exp(-jnp.abs(z))))
            lks.append(jnp.where(causal, lk, 0.0) if diag else lk)
        parts = [_split_bf16(lk) for lk in lks]
        laters = [jnp.dot(hi, tri, preferred_element_type=F32)
                  + jnp.dot(lo, tri, preferred_element_type=F32)
                  for hi, lo in parts]
        out = []
        for hd in heads:
            carry, acc = state[hd]
            a = jnp.exp(zs[hd] + lks[hd] + laters[hd] + carry)
            if diag:
                a = jnp.where(causal, a, 0.0)
            acc = acc + jnp.dot(a.astype(BF16), v_ref[hd, pl.ds(start, t), :],
                                preferred_element_type=F32)
            out.append((carry + jnp.sum(lks[hd], axis=-1, keepdims=True), acc))
        return tuple(out)

    state = tuple((jnp.zeros((t, 1), F32), jnp.zeros((t, HEAD_DIM), F32))
                  for _ in range(SB_HEADS))
    state = block(i, state, True)
    state = lax.fori_loop(0, i, lambda jj, st: block(i - 1 - jj, st, False), state)
    for hd in range(SB_HEADS):
        o_ref[hd] = state[hd][1].astype(BF16)


def _sb(qb, kb, vb, batch, seq):
    n = batch * seq
    nq = seq // SB_T
    q_spec = pl.BlockSpec((SB_HEADS, SB_T, HEAD_DIM), lambda b, h, i: (h, b * nq + i, 0))
    kv_spec = pl.BlockSpec((SB_HEADS, seq, HEAD_DIM), lambda b, h, i: (h, b, 0))
    return pl.pallas_call(
        _sb_kernel,
        grid=(batch, N_HEADS // SB_HEADS, nq),
        in_specs=[q_spec, kv_spec, kv_spec],
        out_specs=q_spec,
        out_shape=jax.ShapeDtypeStruct((N_HEADS, n, HEAD_DIM), BF16),
        compiler_params=pltpu.CompilerParams(vmem_limit_bytes=VMEM_LIMIT),
        name="stickbreak",
    )(qb, kb, vb)


def _outproj_kernel(x_ref, oa_ref, ob_ref, ga_ref, gb_ref, wpa_ref, wpb_ref, wo_ref,
                    gffn_ref, wr_hi_ref, wr_lo_ref, br_ref,
                    h_ref, hn_ref, idx_ref, gate_ref, rank_ref, cnt_ref, carry_ref):
    step = pl.program_id(0)
    tm = x_ref.shape[0]

    @pl.when(step == 0)
    def _():
        carry_ref[...] = jnp.zeros_like(carry_ref)

    pa = jnp.dot(oa_ref[0], wpa_ref[0], preferred_element_type=F32)
    pb = jnp.dot(ob_ref[0], wpb_ref[0], preferred_element_type=F32)
    for hd in range(1, N_HEADS):
        pa = pa + jnp.dot(oa_ref[hd], wpa_ref[hd], preferred_element_type=F32)
        pb = pb + jnp.dot(ob_ref[hd], wpb_ref[hd], preferred_element_type=F32)
    merged = jax.nn.sigmoid(ga_ref[...]) * pa + jax.nn.sigmoid(gb_ref[...]) * pb
    h = x_ref[...] + jnp.dot(merged.astype(BF16), wo_ref[...], preferred_element_type=F32)
    h_ref[...] = h
    ms = jnp.mean(h * h, axis=-1, keepdims=True)
    hn = h * lax.rsqrt(ms + RMS_EPS) * gffn_ref[...]
    hn_ref[...] = hn

    hn_hi, hn_lo = _split_bf16(hn)
    logits = (jnp.dot(hn_hi, wr_hi_ref[...], preferred_element_type=F32)
              + jnp.dot(hn_lo, wr_hi_ref[...], preferred_element_type=F32)
              + jnp.dot(hn_hi, wr_lo_ref[...], preferred_element_type=F32)) + br_ref[...]

    lane = lax.broadcasted_iota(I32, (tm, N_EXPERTS), 1)
    lane_k = lax.broadcasted_iota(I32, (tm, TOP_K), 1)
    vals = logits
    picks, top_vals, top_idx = [], [], []
    for _ in range(TOP_K):
        mx = jnp.max(vals, axis=-1, keepdims=True)
        first = jnp.min(jnp.where(vals == mx, lane, N_EXPERTS), axis=-1, keepdims=True)
        pick = lane == first
        picks.append(pick)
        top_vals.append(mx)
        top_idx.append(first)
        vals = jnp.where(pick, -jnp.inf, vals)
    exps = [jnp.exp(v - top_vals[0]) for v in top_vals]
    denom = exps[0] + exps[1] + exps[2] + exps[3]

    any_pick = picks[0] | picks[1] | picks[2] | picks[3]
    onehot = jnp.where(any_pick, 1.0, 0.0)
    r_i = lax.broadcasted_iota(I32, (tm, tm), 0)
    c_i = lax.broadcasted_iota(I32, (tm, tm), 1)
    lower = jnp.where(c_i < r_i, 1.0, 0.0).astype(BF16)
    pos = jnp.dot(lower, onehot.astype(BF16), preferred_element_type=F32) + carry_ref[...]
    carry_ref[...] = carry_ref[...] + jnp.sum(onehot, axis=0, keepdims=True)
    cnt_ref[...] = carry_ref[...].astype(I32)

    idx4 = jnp.zeros((tm, TOP_K), I32)
    gate4 = jnp.zeros((tm, TOP_K), F32)
    rank4 = jnp.zeros((tm, TOP_K), I32)
    for k in range(TOP_K):
        rk = jnp.sum(jnp.where(picks[k], pos, 0.0), axis=-1, keepdims=True).astype(I32)
        idx4 = jnp.where(lane_k == k, top_idx[k], idx4)
        gate4 = jnp.where(lane_k == k, exps[k] / denom, gate4)
        rank4 = jnp.where(lane_k == k, rk, rank4)
    idx_ref[...] = idx4
    gate_ref[...] = gate4
    rank_ref[...] = rank4


def _outproj(x2, oa, ob, ga, gb, w_proj_a, w_proj_b, w_out, norm_ffn_g, w_router, b_router):
    n = x2.shape[0]
    tm = ROW_TILE
    wpa = w_proj_a.astype(BF16).reshape(N_HEADS, HEAD_DIM, D_MODEL)
    wpb = w_proj_b.astype(BF16).reshape(N_HEADS, HEAD_DIM, D_MODEL)
    wr_hi = w_router.astype(BF16)
    wr_lo = (w_router - wr_hi.astype(F32)).astype(BF16)
    full = lambda shape: pl.BlockSpec(shape, lambda i: (0,) * len(shape))
    row = lambda width: pl.BlockSpec((tm, width), lambda i: (i, 0))
    head_spec = pl.BlockSpec((N_HEADS, tm, HEAD_DIM), lambda i: (0, i, 0))
    return pl.pallas_call(
        _outproj_kernel,
        grid=(n // tm,),
        in_specs=[row(D_MODEL), head_spec, head_spec, row(D_MODEL), row(D_MODEL),
                  full((N_HEADS, HEAD_DIM, D_MODEL)), full((N_HEADS, HEAD_DIM, D_MODEL)),
                  full((D_MODEL, D_MODEL)), full((1, D_MODEL)),
                  full((D_MODEL, N_EXPERTS)), full((D_MODEL, N_EXPERTS)), full((1, N_EXPERTS))],
        out_specs=[row(D_MODEL), row(D_MODEL), row(TOP_K), row(TOP_K), row(TOP_K),
                   full((1, N_EXPERTS))],
        out_shape=[jax.ShapeDtypeStruct((n, D_MODEL), F32), jax.ShapeDtypeStruct((n, D_MODEL), F32),
                   jax.ShapeDtypeStruct((n, TOP_K), I32), jax.ShapeDtypeStruct((n, TOP_K), F32),
                   jax.ShapeDtypeStruct((n, TOP_K), I32), jax.ShapeDtypeStruct((1, N_EXPERTS), I32)],
        scratch_shapes=[pltpu.VMEM((1, N_EXPERTS), F32)],
        compiler_params=pltpu.CompilerParams(vmem_limit_bytes=VMEM_LIMIT,
                                             dimension_semantics=("arbitrary",)),
        name="outproj_router",
    )(x2, oa, ob, ga, gb, wpa, wpb, w_out.astype(BF16), norm_ffn_g[None, :],
      wr_hi, wr_lo, b_router[None, :])


def _dispatch_kernel(dest_ref, fill_lo_ref, fill_hi_ref, hn_ref, xs_ref, zero_ref, sem, zsem):
    s = pl.program_id(0)
    dt = hn_ref.shape[0]

    def tok(tt, carry):
        t = s * dt + tt
        for k in range(TOP_K):
            d = dest_ref[t * TOP_K + k]
            pltpu.make_async_copy(hn_ref.at[pl.ds(tt, 1)], xs_ref.at[pl.ds(d, 1)],
                                  sem).start(priority=k % 2)
        return carry
    lax.fori_loop(0, dt, tok, 0, unroll=4)

    @pl.when(s == 0)
    def _():
        zero_ref[...] = jnp.zeros_like(zero_ref)

        def expert(e, total):
            lo, hi = fill_lo_ref[e], fill_hi_ref[e]

            def fill(r, carry):
                pltpu.make_async_copy(zero_ref, xs_ref.at[pl.ds(r, 1)], zsem).start()
                return carry
            lax.fori_loop(lo, hi, fill, 0)
            return total + (hi - lo)
        total = lax.fori_loop(0, fill_lo_ref.shape[0], expert, jnp.int32(0))

        def drain(r, carry):
            pltpu.make_async_copy(zero_ref, xs_ref.at[pl.ds(0, 1)], zsem).wait()
            return carry
        lax.fori_loop(0, total, drain, 0)

    for _ in range(TOP_K):
        pltpu.make_async_copy(hn_ref, xs_ref.at[pl.ds(0, dt)], sem).wait()


def _dispatch(dest_flat, fill_lo, fill_hi, hn, rows):
    n = hn.shape[0]
    dt = DISPATCH_T
    return pl.pallas_call(
        _dispatch_kernel,
        grid_spec=pltpu.PrefetchScalarGridSpec(
            num_scalar_prefetch=3,
            grid=(n // dt,),
            in_specs=[pl.BlockSpec((dt, D_MODEL), lambda i, d, lo, hi: (i, 0))],
            out_specs=pl.BlockSpec(memory_space=pl.ANY),
            scratch_shapes=[pltpu.VMEM((1, D_MODEL), F32),
                            pltpu.SemaphoreType.DMA(()), pltpu.SemaphoreType.DMA(())]),
        out_shape=jax.ShapeDtypeStruct((rows, D_MODEL), F32),
        compiler_params=pltpu.CompilerParams(vmem_limit_bytes=VMEM_LIMIT,
                                             dimension_semantics=("arbitrary",),
                                             has_side_effects=True),
        name="dispatch",
    )(dest_flat, fill_lo, fill_hi, hn)


def _expert_kernel(be_ref, nu_ref, x_ref, wgu_ref, bgu_ref, wd_ref, bd_ref, y_ref,
                   wgu_bf, wd_bf):
    b = pl.program_id(0)
    n_used = nu_ref[0]
    prev = be_ref[jnp.maximum(b - 1, 0)]

    @pl.when((b < n_used) & ((b == 0) | (be_ref[b] != prev)))
    def _():
        wgu_bf[...] = wgu_ref[0].astype(BF16)
        wd_bf[...] = wd_ref[0].astype(BF16)

    @pl.when(b < n_used)
    def _():
        xb = x_ref[...].astype(BF16)
        hgu = jnp.dot(xb, wgu_bf[...], preferred_element_type=F32) + bgu_ref[0]
        gate = jnp.minimum(hgu[:, :D_EXPERT], SWIGLU_LIMIT)
        up = jnp.clip(hgu[:, D_EXPERT:], -SWIGLU_LIMIT, SWIGLU_LIMIT)
        act = (up + 1.0) * gate * jax.nn.sigmoid(SWIGLU_ALPHA * gate)
        y_ref[...] = (jnp.dot(act.astype(BF16), wd_bf[...], preferred_element_type=F32)
                      + bd_ref[0])

    @pl.when(b >= n_used)
    def _():
        y_ref[...] = jnp.zeros_like(y_ref)


def _experts(block_expert, n_used, xs, w_gu, b_gu, w_down, b_down):
    rows = xs.shape[0]
    n_blocks = rows // EXPERT_BLOCK

    def blk(b, be, nu):
        return jnp.minimum(b, nu[0] - 1)

    w_spec = lambda d1, d2: pl.BlockSpec((1, d1, d2), lambda b, be, nu: (be[blk(b, be, nu)], 0, 0))
    return pl.pallas_call(
        _expert_kernel,
        grid_spec=pltpu.PrefetchScalarGridSpec(
            num_scalar_prefetch=2,
            grid=(n_blocks,),
            in_specs=[pl.BlockSpec((EXPERT_BLOCK, D_MODEL), lambda b, be, nu: (blk(b, be, nu), 0)),
                      w_spec(D_MODEL, 2 * D_EXPERT), w_spec(1, 2 * D_EXPERT),
                      w_spec(D_EXPERT, D_MODEL), w_spec(1, D_MODEL)],
            out_specs=pl.BlockSpec((EXPERT_BLOCK, D_MODEL), lambda b, be, nu: (b, 0)),
            scratch_shapes=[pltpu.VMEM((D_MODEL, 2 * D_EXPERT), BF16),
                            pltpu.VMEM((D_EXPERT, D_MODEL), BF16)]),
        out_shape=jax.ShapeDtypeStruct((rows, D_MODEL), F32),
        compiler_params=pltpu.CompilerParams(vmem_limit_bytes=VMEM_LIMIT,
                                             dimension_semantics=("arbitrary",)),
        name="experts",
    )(block_expert, n_used, xs, w_gu, b_gu[:, None, :], w_down, b_down[:, None, :])


def _combine_kernel(dest_ref, h_ref, gate_ref, y_ref, o_ref, buf_ref, sems):
    s = pl.program_id(0)
    n_steps = pl.num_programs(0)
    ct = COMBINE_T

    def issue(step, slot):
        def tok(tt, carry):
            t = step * ct + tt
            for k in range(TOP_K):
                d = dest_ref[t * TOP_K + k]
                pltpu.make_async_copy(y_ref.at[pl.ds(d, 1)], buf_ref.at[slot, k, pl.ds(tt, 1)],
                                      sems.at[slot]).start(priority=k % 2)
            return carry
        lax.fori_loop(0, ct, tok, 0, unroll=4)

    @pl.when(s == 0)
    def _():
        issue(0, 0)

    for parity in range(2):
        @pl.when((s + 1 < n_steps) & ((s + 1) % 2 == parity))
        def _():
            issue(s + 1, parity)

    slot = s % 2
    for k in range(TOP_K):
        pltpu.make_async_copy(y_ref.at[pl.ds(0, ct)], buf_ref.at[slot, k], sems.at[slot]).wait()
    g = gate_ref[...]
    out = h_ref[...]
    for k in range(TOP_K):
        out = out + g[:, k:k + 1] * buf_ref[slot, k]
    o_ref[...] = out


def _combine(dest_flat, h, gates, y):
    n = h.shape[0]
    ct = COMBINE_T
    return pl.pallas_call(
        _combine_kernel,
        grid_spec=pltpu.PrefetchScalarGridSpec(
            num_scalar_prefetch=1,
            grid=(n // ct,),
            in_specs=[pl.BlockSpec((ct, D_MODEL), lambda i, d: (i, 0)),
                      pl.BlockSpec((ct, TOP_K), lambda i, d: (i, 0)),
                      pl.BlockSpec(memory_space=pl.ANY)],
            out_specs=pl.BlockSpec((ct, D_MODEL), lambda i, d: (i, 0)),
            scratch_shapes=[pltpu.VMEM((2, TOP_K, ct, D_MODEL), F32),
                            pltpu.SemaphoreType.DMA((2,))]),
        out_shape=jax.ShapeDtypeStruct((n, D_MODEL), F32),
        compiler_params=pltpu.CompilerParams(vmem_limit_bytes=VMEM_LIMIT,
                                             dimension_semantics=("arbitrary",)),
        name="combine",
    )(dest_flat, h, gates, y)


def _moe(h, hn, top_idx, gates, rank, counts, w_gate_up, b_gate_up, w_down, b_down):
    n = h.shape[0]
    m = n * TOP_K
    n_blocks = (m + N_EXPERTS * (EXPERT_BLOCK - 1) + EXPERT_BLOCK - 1) // EXPERT_BLOCK
    rows = n_blocks * EXPERT_BLOCK
    counts = counts[0]
    padded = ((counts + EXPERT_BLOCK - 1) // EXPERT_BLOCK) * EXPERT_BLOCK
    pend = jnp.cumsum(padded).astype(I32)
    pstart = pend - padded
    dest = (pstart[top_idx] + rank).reshape(m).astype(I32)
    block_start = jnp.arange(n_blocks, dtype=I32) * EXPERT_BLOCK
    block_expert = jnp.minimum(jnp.sum(pend[None, :] <= block_start[:, None], axis=1),
                               N_EXPERTS - 1).astype(I32)
    n_used = (pend[-1:] // EXPERT_BLOCK).astype(I32)
    fill_lo = jnp.concatenate([pstart + counts, pend[-1:]]).astype(I32)
    fill_hi = jnp.concatenate([pend, jnp.full((1,), rows, I32)])
    xs = _dispatch(dest, fill_lo, fill_hi, hn, rows)
    y = _experts(block_expert, n_used, xs, w_gate_up, b_gate_up, w_down, b_down)
    return _combine(dest, h, gates, y)


def kernel(x, norm_mix_g, w_in, q_norm_g, k_norm_g, w_proj_a, w_proj_b, w_out, norm_ffn_g,
           w_router, b_router, w_gate_up, b_gate_up, w_down, b_down):
    batch, seq, d = x.shape
    h = x.reshape(batch * seq, d)
    for l in range(norm_mix_g.shape[0]):
        (qa, qi, qb, kb, vb, ka, ki, va, wi, ga, gb) = _inproj(
            h, norm_mix_g[l], w_in[l], q_norm_g[l], k_norm_g[l], seq)
        oa = _dsa(qa, qi, wi, ka, ki, va, batch, seq)
        ob = _sb(qb, kb, vb, batch, seq)
        h_mid, hn, top_idx, gates, rank, counts = _outproj(
            h, oa, ob, ga, gb, w_proj_a[l], w_proj_b[l], w_out[l], norm_ffn_g[l],
            w_router[l], b_router[l])
        h = _moe(h_mid, hn, top_idx, gates, rank, counts,
                 w_gate_up[l], b_gate_up[l], w_down[l], b_down[l])
    return h.reshape(batch, seq, d)
```

```python
import functools

import jax
import jax.numpy as jnp
from jax import lax
from jax.experimental import pallas as pl
from jax.experimental.pallas import tpu as pltpu

F32 = jnp.float32
BF16 = jnp.bfloat16
I32 = jnp.int32

D_MODEL = 1024
CHUNK = 64
HEAD_DIM = 64
N_HEADS = 8
MAX_SELECTED_KEYS = 256
ROT_DIM = HEAD_DIM // 4
ROPE_THETA = 500000.0
N_EXPERTS = 32
TOP_K = 4
D_EXPERT = D_MODEL
SWIGLU_LIMIT = 7.0
SWIGLU_ALPHA = 1.702
EXPERT_BLOCK = 256
RMS_EPS = 1e-6

LANES = 128
VMEM_LIMIT = 56 * 1024 * 1024

ROW_TILE = 256
DSA_QB = 128
DSA_KC = 512
SB_T = 256
SB_HEADS = 8
COMBINE_T = 128
DISPATCH_T = 512

NEG_BIG = -1e30
TINY = 1e-30


def _nt_dot(a, b):
    return lax.dot_general(a, b, (((1,), (1,)), ((), ())), preferred_element_type=F32)


def _split_bf16(x):
    hi = x.astype(BF16)
    lo = (x - hi.astype(F32)).astype(BF16)
    return hi, lo


_OFF_QA, _OFF_QI, _OFF_QB, _OFF_KB, _OFF_VB = 0, 512, 1024, 1536, 2048
_OFF_KK, _OFF_VW, _OFF_GA, _OFF_GB, _W_COLS = 2560, 2688, 2816, 3840, 4864


def _rope128(y, c, s1, s2):
    return y * c + pltpu.roll(y, 8, 1) * s1 + pltpu.roll(y, LANES - 8, 1) * s2


def _head_rms_inv(x, norm_lo, norm_hi):
    lane = lax.broadcasted_iota(I32, x.shape, 1)
    lo = lane < HEAD_DIM
    sq = x * x
    one = jnp.ones((x.shape[0], 1), F32)
    inv_lo = one
    inv_hi = one
    if norm_lo:
        s = jnp.sum(jnp.where(lo, sq, 0.0), axis=-1, keepdims=True)
        inv_lo = lax.rsqrt(s * (1.0 / HEAD_DIM) + RMS_EPS)
    if norm_hi:
        s = jnp.sum(jnp.where(lo, 0.0, sq), axis=-1, keepdims=True)
        inv_hi = lax.rsqrt(s * (1.0 / HEAD_DIM) + RMS_EPS)
    return jnp.where(lo, inv_lo, inv_hi)


def _inproj_kernel(x_ref, gmix_ref, w_ref, wvt_ref, qg_ref, kg_ref, c_ref, s1_ref, s2_ref,
                   qa_ref, qi_ref, qb_ref, kb_ref, vb_ref, ka_ref, ki_ref, va_ref,
                   wi_ref, ga_ref, gb_ref):
    x = x_ref[...]
    ms = jnp.mean(x * x, axis=-1, keepdims=True)
    xn = (x * lax.rsqrt(ms + RMS_EPS) * gmix_ref[...]).astype(BF16)

    def proj(off, width):
        return jnp.dot(xn, w_ref[:, off:off + width], preferred_element_type=F32)

    c, s1, s2 = c_ref[...], s1_ref[...], s2_ref[...]
    scale = HEAD_DIM ** -0.5

    def put_heads(dst_ref, pair, val):
        dst_ref[2 * pair] = val[:, :HEAD_DIM].astype(BF16)
        dst_ref[2 * pair + 1] = val[:, HEAD_DIM:].astype(BF16)

    acc = proj(_OFF_QA, 512)
    for p in range(4):
        blk = acc[:, p * LANES:(p + 1) * LANES]
        y = blk * _head_rms_inv(blk, True, True) * qg_ref[...]
        put_heads(qa_ref, p, _rope128(y, c, s1, s2) * scale)

    acc = proj(_OFF_QI, 512)
    for p in range(4):
        blk = acc[:, p * LANES:(p + 1) * LANES]
        put_heads(qi_ref, p, _rope128(blk, c, s1, s2) * scale)

    acc = proj(_OFF_QB, 512)
    for p in range(4):
        put_heads(qb_ref, p, acc[:, p * LANES:(p + 1) * LANES] * scale)
    acc = proj(_OFF_KB, 512)
    for p in range(4):
        put_heads(kb_ref, p, acc[:, p * LANES:(p + 1) * LANES])
    acc = proj(_OFF_VB, 512)
    for p in range(4):
        put_heads(vb_ref, p, acc[:, p * LANES:(p + 1) * LANES])

    kk = proj(_OFF_KK, LANES)
    y = kk * _head_rms_inv(kk, True, False) * kg_ref[...]
    y = _rope128(y, c, s1, s2)
    ka_ref[...] = y[:, :HEAD_DIM].astype(BF16)
    ki_ref[...] = y[:, HEAD_DIM:].astype(BF16)

    vw_t = _nt_dot(wvt_ref[...], xn)
    va_ref[0] = vw_t[:HEAD_DIM].astype(BF16)
    wi_ref[...] = vw_t[HEAD_DIM:HEAD_DIM + N_HEADS]

    ga_ref[...] = proj(_OFF_GA, D_MODEL)
    gb_ref[...] = proj(_OFF_GB, D_MODEL)


def _rope_tables(seq):
    pos = jnp.arange(seq, dtype=F32)
    inv_freq = ROPE_THETA ** (-jnp.arange(0, ROT_DIM, 2, dtype=F32) / ROT_DIM)
    ang = pos[:, None] * inv_freq[None, :]
    cos, sin = jnp.cos(ang), jnp.sin(ang)
    half = ROT_DIM // 2
    zeros = jnp.zeros((seq, HEAD_DIM - ROT_DIM), F32)
    zh = jnp.zeros((seq, half), F32)
    c64 = jnp.concatenate([cos, cos, zeros + 1.0], axis=1)
    s1_64 = jnp.concatenate([zh, sin, zeros], axis=1)
    s2_64 = jnp.concatenate([-sin, zh, zeros], axis=1)
    tile2 = lambda t: jnp.concatenate([t, t], axis=1)
    return tile2(c64), tile2(s1_64), tile2(s2_64)


def _inproj(x2, norm_g, w_in, q_norm_g, k_norm_g, seq):
    n = x2.shape[0]
    w = w_in
    sl = lambda a, b: w[:, a:b]
    o = [0, 512, 576, 640, 1152, 1216, 1224, 1736, 2248, 2760, 3784, 4808]
    pad = jnp.zeros((D_MODEL, LANES - HEAD_DIM - N_HEADS), w.dtype)
    w_all = jnp.concatenate([
        sl(o[0], o[1]), sl(o[3], o[4]), sl(o[6], o[7]), sl(o[7], o[8]), sl(o[8], o[9]),
        sl(o[1], o[2]), sl(o[4], o[5]),
        sl(o[2], o[3]), sl(o[5], o[6]), pad,
        sl(o[9], o[10]), sl(o[10], o[11])], axis=1).astype(BF16)
    assert w_all.shape[1] == _W_COLS
    qg = jnp.concatenate([q_norm_g, q_norm_g])[None, :]
    kg = jnp.concatenate([k_norm_g, jnp.ones_like(k_norm_g)])[None, :]
    c, s1, s2 = _rope_tables(seq)
    tm = ROW_TILE
    pos_blocks = seq // tm
    full = lambda shape: pl.BlockSpec(shape, lambda i: (0,) * len(shape))
    head_out = jax.ShapeDtypeStruct((N_HEADS, n, HEAD_DIM), BF16)
    head_spec = pl.BlockSpec((N_HEADS, tm, HEAD_DIM), lambda i: (0, i, 0))
    row64 = jax.ShapeDtypeStruct((n, HEAD_DIM), BF16)
    row64_spec = pl.BlockSpec((tm, HEAD_DIM), lambda i: (i, 0))
    tab_spec = pl.BlockSpec((tm, LANES), lambda i: (i % pos_blocks, 0))
    return pl.pallas_call(
        _inproj_kernel,
        grid=(n // tm,),
        in_specs=[pl.BlockSpec((tm, D_MODEL), lambda i: (i, 0)),
                  full((1, D_MODEL)), full((D_MODEL, _W_COLS)), full((LANES, D_MODEL)),
                  full((1, LANES)), full((1, LANES)), tab_spec, tab_spec, tab_spec],
        out_specs=[head_spec] * 5 + [row64_spec] * 2
                  + [pl.BlockSpec((1, HEAD_DIM, tm), lambda i: (i, 0, 0)),
                     pl.BlockSpec((N_HEADS, tm), lambda i: (0, i)),
                     pl.BlockSpec((tm, D_MODEL), lambda i: (i, 0)),
                     pl.BlockSpec((tm, D_MODEL), lambda i: (i, 0))],
        out_shape=[head_out] * 5 + [row64] * 2
                  + [jax.ShapeDtypeStruct((n // tm, HEAD_DIM, tm), BF16),
                     jax.ShapeDtypeStruct((N_HEADS, n), F32),
                     jax.ShapeDtypeStruct((n, D_MODEL), F32),
                     jax.ShapeDtypeStruct((n, D_MODEL), F32)],
        compiler_params=pltpu.CompilerParams(vmem_limit_bytes=VMEM_LIMIT),
        name="inproj",
    )(x2, norm_g[None, :], w_all, w_all[:, _OFF_VW:_OFF_VW + LANES].T, qg, kg, c, s1, s2)


def _key_to_float(u):
    k = u ^ jnp.int32(-2 ** 31)
    b = k ^ ((k >> 31) & jnp.int32(0x7FFFFFFF))
    return lax.bitcast_convert_type(b, F32)


def _float_to_key(x):
    b = lax.bitcast_convert_type(x, I32)
    return b ^ ((b >> 31) & jnp.int32(0x7FFFFFFF)) ^ jnp.int32(-2 ** 31)


def _bit_transpose32(words):
    a = list(words)
    j, m = 16, 0x0000FFFF
    while j:
        k = 0
        while k < 32:
            t = (a[k] ^ lax.shift_right_logical(a[k + j], jnp.int32(j))) & jnp.int32(_as_i32(m))
            a[k] = a[k] ^ t
            a[k + j] = a[k + j] ^ (t << j)
            k = (k + j + 1) & ~j
        j >>= 1
        m = (m ^ (m << j)) & 0xFFFFFFFF
    return a[::-1]


def _as_i32(v):
    return v - (1 << 32) if v >= (1 << 31) else v


def _slab_reduce(x, op):
    parts = [x[j * 8:(j + 1) * 8] for j in range(x.shape[0] // 8)]
    while len(parts) > 1:
        nxt = [op(parts[j], parts[j + 1]) for j in range(0, len(parts) - 1, 2)]
        if len(parts) % 2:
            nxt.append(parts[-1])
        parts = nxt
    return parts[0]


def _dsa_kernel(n_sel, qa_ref, qi_ref, wi_ref, ka_ref, ki_ref, va_ref, o_ref,
                score_ref, planes_ref, alive_ref, m_ref, l_ref, acc_ref):
    i = pl.program_id(1)
    qb, kc = DSA_QB, DSA_KC
    pairs = range(N_HEADS // 2)
    va_pieces = kc // ROW_TILE
    n_chunks = (i * qb + qb + kc - 1) // kc
    t_pos = i * qb + lax.broadcasted_iota(I32, (1, qb), 1)
    limit = (t_pos // CHUNK + 1) * CHUNK
    kio = lax.broadcasted_iota(I32, (kc, qb), 0)

    def pair_rows(ref, p):
        return ref[2 * p:2 * p + 2].reshape(2 * qb, HEAD_DIM)

    def twice(x):
        return jnp.concatenate([x, x], axis=1)

    w = wi_ref[...] * (N_HEADS ** -0.5)
    w2 = [jnp.concatenate([w[2 * p:2 * p + 1], w[2 * p + 1:2 * p + 2]], axis=1) for p in pairs]
    q_idx = [pair_rows(qi_ref, p) for p in pairs]
    groups_per_chunk = kc // (8 * 32)
    n_groups = planes_ref.shape[1]

    def idx_body(c, carry):
        start = pl.multiple_of(c * kc, kc)
        keys = ki_ref[pl.ds(start, kc), :]
        tot = None
        for p in pairs:
            t = jnp.maximum(_nt_dot(keys, q_idx[p]), 0.0) * w2[p]
            tot = t if tot is None else tot + t
        sc = jnp.where(start + kio < limit, tot[:, :qb] + tot[:, qb:], -jnp.inf)
        score_ref[c] = sc
        okey = _float_to_key(sc)
        for g in range(groups_per_chunk):
            slabs = [okey[(g * 32 + j) * 8:(g * 32 + j + 1) * 8] for j in range(32)]
            planes = _bit_transpose32(slabs)
            for bit in range(32):
                planes_ref[bit, groups_per_chunk * c + g] = planes[bit]
        return carry

    lax.fori_loop(0, n_chunks, idx_body, 0)

    def count(pred_fn):
        def body(c, acc):
            return acc + _slab_reduce(pred_fn(c).astype(F32), jnp.add)
        acc = lax.fori_loop(0, n_chunks, body, jnp.zeros((8, qb), F32))
        return jnp.sum(acc, axis=0, keepdims=True)

    def kth_largest(val_fn):
        def body(it, carry):
            u, n_ge = carry
            trial = u | (jnp.int32(1) << (31 - it))
            cand = _key_to_float(trial)
            cnt = count(lambda c: val_fn(c) >= cand)
            take = cnt >= n_sel
            return jnp.where(take, trial, u), jnp.where(take, cnt, n_ge)
        u, n_ge = lax.fori_loop(0, 32, body,
                                (jnp.zeros((1, qb), I32), jnp.full((1, qb), n_sel, F32)))
        return _key_to_float(u), n_ge

    def kth_largest_key():
        used = groups_per_chunk * n_chunks
        for g in range(n_groups):
            @pl.when(g >= used)
            def _():
                for bit in range(32):
                    planes_ref[bit, g] = jnp.zeros((8, qb), I32)
            alive_ref[g] = jnp.where(g < used, jnp.int32(-1), jnp.int32(0)) + jnp.zeros((8, qb), I32)

        def body(it, carry):
            u, want = carry
            bit = 31 - it
            cnt8 = jnp.zeros((8, qb), I32)
            for g in range(n_groups):
                cnt8 = cnt8 + lax.population_count(alive_ref[g] & planes_ref[bit, g])
            cnt = jnp.sum(cnt8, axis=0, keepdims=True)
            take = cnt >= want
            flip = jnp.where(take, jnp.int32(0), jnp.int32(-1))
            for g in range(n_groups):
                alive_ref[g] = alive_ref[g] & (planes_ref[bit, g] ^ flip)
            return (jnp.where(take, u | (jnp.int32(1) << bit), u),
                    jnp.where(take, want, want - cnt))
        u, _ = lax.fori_loop(0, 32, body,
                             (jnp.zeros((1, qb), I32), jnp.full((1, qb), n_sel, I32)))
        return _key_to_float(u)

    select_all = limit <= n_sel
    thr = kth_largest_key()
    n_ge = count(lambda c: score_ref[c] >= thr)
    n_gt = count(lambda c: score_ref[c] > thr)
    settled = select_all | ((n_gt < n_sel) & (n_ge >= n_sel))
    thr, n_ge = lax.cond(jnp.min(jnp.where(settled, 1, 0)) > 0,
                         lambda: (thr, n_ge), lambda: kth_largest(lambda c: score_ref[c]))
    thr = jnp.where(select_all, -jnp.inf, thr)

    def to_excess(c, carry):
        s = score_ref[c]
        above = jnp.where(s > thr, jnp.maximum(s - thr, TINY), 0.0)
        score_ref[c] = jnp.where(s >= thr, above, -jnp.inf)
        return carry
    lax.fori_loop(0, n_chunks, to_excess, 0)

    def refine():
        n_above = count(lambda c: score_ref[c] > 0.0)
        crowded = jnp.max(jnp.where((n_above >= n_sel) & jnp.logical_not(select_all), 1, 0)) > 0

        def split():
            t2, _ = kth_largest(lambda c: score_ref[c])
            return t2, n_sel - count(lambda c: score_ref[c] > t2)
        thr2, need = lax.cond(crowded, split, lambda: (jnp.zeros((1, qb), F32), n_sel - n_above))

        def tie_body(it, j):
            trial = j | (jnp.int32(1) << (12 - it))
            g = count(lambda c: (score_ref[c] == thr2) & (c * kc + kio < trial))
            return jnp.where(g <= need, trial, j)
        return thr2, lax.fori_loop(0, 13, tie_body, jnp.zeros((1, qb), I32))

    tied = jnp.max(jnp.where((n_ge > n_sel) & jnp.logical_not(select_all), 1, 0)) > 0
    thr2, j_lim = lax.cond(tied, refine, lambda: (jnp.zeros((1, qb), F32),
                                                  jnp.full((1, qb), 2 ** 30, I32)))
    thr2 = jnp.where(select_all, 0.0, thr2)
    j_lim = jnp.where(select_all, jnp.int32(2 ** 30), j_lim)

    q_att = [pair_rows(qa_ref, p) for p in pairs]
    m_ref[...] = jnp.full_like(m_ref, NEG_BIG)
    l_ref[...] = jnp.zeros_like(l_ref)
    acc_ref[...] = jnp.zeros_like(acc_ref)

    def att_body(c, carry):
        start = pl.multiple_of(c * kc, kc)
        kpos = start + kio
        ex = score_ref[c]
        sel = (kpos < limit) & ((ex > thr2) | ((ex == thr2) & (kpos < j_lim)))
        bias = twice(jnp.where(sel, 0.0, NEG_BIG))
        keys = ka_ref[pl.ds(start, kc), :]
        s = [_nt_dot(keys, q_att[p]) + bias for p in pairs]
        m_old = [m_ref[p] for p in pairs]
        m_new = [jnp.maximum(m_old[p], jnp.max(_slab_reduce(s[p], jnp.maximum), axis=0,
                                               keepdims=True)) for p in pairs]
        alpha = [jnp.exp(m_old[p] - m_new[p]) for p in pairs]
        pe = [jnp.exp(s[p] - m_new[p]) for p in pairs]
        pv = []
        for p in pairs:
            pb = pe[p].astype(BF16)
            tot = None
            for v in range(va_pieces):
                part = jnp.dot(va_ref[va_pieces * c + v], pb[v * ROW_TILE:(v + 1) * ROW_TILE],
                               preferred_element_type=F32)
                tot = part if tot is None else tot + part
            pv.append(tot)
        for p in pairs:
            l_ref[p] = alpha[p] * l_ref[p] + jnp.sum(_slab_reduce(pe[p], jnp.add), axis=0,
                                                     keepdims=True)
            acc_ref[p] = acc_ref[p] * alpha[p] + pv[p]
            m_ref[p] = m_new[p]
        return carry

    lax.fori_loop(0, n_chunks, att_body, 0)
    for p in pairs:
        out = acc_ref[p] / l_ref[p]
        o_ref[2 * p] = out[:, :qb].T.astype(BF16)
        o_ref[2 * p + 1] = out[:, qb:].T.astype(BF16)


def _dsa(qa, qi, wi, ka, ki, va, batch, seq):
    n = batch * seq
    nqb = seq // DSA_QB
    n_sel = min(MAX_SELECTED_KEYS, seq // 4)
    head_spec = pl.BlockSpec((N_HEADS, DSA_QB, HEAD_DIM), lambda b, i: (0, b * nqb + i, 0))
    kv_spec = pl.BlockSpec((seq, HEAD_DIM), lambda b, i: (b, 0))
    pair_stat = pltpu.VMEM((N_HEADS // 2, 1, 2 * DSA_QB), F32)
    return pl.pallas_call(
        functools.partial(_dsa_kernel, n_sel),
        grid=(batch, nqb),
        in_specs=[head_spec, head_spec,
                  pl.BlockSpec((N_HEADS, DSA_QB), lambda b, i: (0, b * nqb + i)),
                  kv_spec, kv_spec,
                  pl.BlockSpec((seq // ROW_TILE, HEAD_DIM, ROW_TILE), lambda b, i: (b, 0, 0))],
        out_specs=head_spec,
        out_shape=jax.ShapeDtypeStruct((N_HEADS, n, HEAD_DIM), BF16),
        scratch_shapes=[pltpu.VMEM((seq // DSA_KC, DSA_KC, DSA_QB), F32),
                        pltpu.VMEM((32, seq // 256, 8, DSA_QB), I32),
                        pltpu.VMEM((seq // 256, 8, DSA_QB), I32), pair_stat, pair_stat,
                        pltpu.VMEM((N_HEADS // 2, HEAD_DIM, 2 * DSA_QB), F32)],
        compiler_params=pltpu.CompilerParams(vmem_limit_bytes=VMEM_LIMIT),
        name="dsa",
    )(qa, qi, wi, ka, ki, va)


def _sb_kernel(q_ref, k_ref, v_ref, o_ref):
    i = pl.program_id(2)
    t = SB_T
    row = lax.broadcasted_iota(I32, (t, t), 0)
    col = lax.broadcasted_iota(I32, (t, t), 1)
    tri = jnp.where(row > col, 1.0, 0.0).astype(BF16)
    causal = col < row

    def block(j, state, diag):
        heads = range(SB_HEADS)
        start = pl.multiple_of(j * t, t)
        zs = [_nt_dot(q_ref[hd], k_ref[hd, pl.ds(start, t), :]) for hd in heads]
        lks = []
        for z in zs:
            lk = -(jnp.maximum(z, 0.0) + jnp.log(1.0 + jnp.exp(-jnp.abs(z))))
            lks.append(jnp.where(causal, lk, 0.0) if diag else lk)
        parts = [_split_bf16(lk) for lk in lks]
        laters = [jnp.dot(hi, tri, preferred_element_type=F32)
                  + jnp.dot(lo, tri, preferred_element_type=F32)
                  for hi, lo in parts]
        out = []
        for hd in heads:
            carry, acc = state[hd]
            a = jnp.exp(zs[hd] + lks[hd] + laters[hd] + carry)
            if diag:
                a = jnp.where(causal, a, 0.0)
            acc = acc + jnp.dot(a.astype(BF16), v_ref[hd, pl.ds(start, t), :],
                                preferred_element_type=F32)
            out.append((carry + jnp.sum(lks[hd], axis=-1, keepdims=True), acc))
        return tuple(out)

    state = tuple((jnp.zeros((t, 1), F32), jnp.zeros((t, HEAD_DIM), F32))
                  for _ in range(SB_HEADS))
    state = block(i, state, True)
    state = lax.fori_loop(0, i, lambda jj, st: block(i - 1 - jj, st, False), state)
    for hd in range(SB_HEADS):
        o_ref[hd] = state[hd][1].astype(BF16)


def _sb(qb, kb, vb, batch, seq):
    n = batch * seq
    nq = seq // SB_T
    q_spec = pl.BlockSpec((SB_HEADS, SB_T, HEAD_DIM), lambda b, h, i: (h, b * nq + i, 0))
    kv_spec = pl.BlockSpec((SB_HEADS, seq, HEAD_DIM), lambda b, h, i: (h, b, 0))
    return pl.pallas_call(
        _sb_kernel,
        grid=(batch, N_HEADS // SB_HEADS, nq),
        in_specs=[q_spec, kv_spec, kv_spec],
        out_specs=q_spec,
        out_shape=jax.ShapeDtypeStruct((N_HEADS, n, HEAD_DIM), BF16),
        compiler_params=pltpu.CompilerParams(vmem_limit_bytes=VMEM_LIMIT),
        name="stickbreak",
    )(qb, kb, vb)


def _outproj_kernel(x_ref, oa_ref, ob_ref, ga_ref, gb_ref, wpa_ref, wpb_ref, wo_ref,
                    gffn_ref, wr_hi_ref, wr_lo_ref, br_ref,
                    h_ref, hn_ref, idx_ref, gate_ref, rank_ref, cnt_ref, carry_ref):
    step = pl.program_id(0)
    tm = x_ref.shape[0]

    @pl.when(step == 0)
    def _():
        carry_ref[...] = jnp.zeros_like(carry_ref)

    pa = jnp.dot(oa_ref[0], wpa_ref[0], preferred_element_type=F32)
    pb = jnp.dot(ob_ref[0], wpb_ref[0], preferred_element_type=F32)
    for hd in range(1, N_HEADS):
        pa = pa + jnp.dot(oa_ref[hd], wpa_ref[hd], preferred_element_type=F32)
        pb = pb + jnp.dot(ob_ref[hd], wpb_ref[hd], preferred_element_type=F32)
    merged = jax.nn.sigmoid(ga_ref[...]) * pa + jax.nn.sigmoid(gb_ref[...]) * pb
    h = x_ref[...] + jnp.dot(merged.astype(BF16), wo_ref[...], preferred_element_type=F32)
    h_ref[...] = h
    ms = jnp.mean(h * h, axis=-1, keepdims=True)
    hn = h * lax.rsqrt(ms + RMS_EPS) * gffn_ref[...]
    hn_ref[...] = hn

    hn_hi, hn_lo = _split_bf16(hn)
    logits = (jnp.dot(hn_hi, wr_hi_ref[...], preferred_element_type=F32)
              + jnp.dot(hn_lo, wr_hi_ref[...], preferred_element_type=F32)
              + jnp.dot(hn_hi, wr_lo_ref[...], preferred_element_type=F32)) + br_ref[...]

    lane = lax.broadcasted_iota(I32, (tm, N_EXPERTS), 1)
    lane_k = lax.broadcasted_iota(I32, (tm, TOP_K), 1)
    vals = logits
    picks, top_vals, top_idx = [], [], []
    for _ in range(TOP_K):
        mx = jnp.max(vals, axis=-1, keepdims=True)
        first = jnp.min(jnp.where(vals == mx, lane, N_EXPERTS), axis=-1, keepdims=True)
        pick = lane == first
        picks.append(pick)
        top_vals.append(mx)
        top_idx.append(first)
        vals = jnp.where(pick, -jnp.inf, vals)
    exps = [jnp.exp(v - top_vals[0]) for v in top_vals]
    denom = exps[0] + exps[1] + exps[2] + exps[3]

    any_pick = picks[0] | picks[1] | picks[2] | picks[3]
    onehot = jnp.where(any_pick, 1.0, 0.0)
    r_i = lax.broadcasted_iota(I32, (tm, tm), 0)
    c_i = lax.broadcasted_iota(I32, (tm, tm), 1)
    lower = jnp.where(c_i < r_i, 1.0, 0.0).astype(BF16)
    pos = jnp.dot(lower, onehot.astype(BF16), preferred_element_type=F32) + carry_ref[...]
    carry_ref[...] = carry_ref[...] + jnp.sum(onehot, axis=0, keepdims=True)
    cnt_ref[...] = carry_ref[...].astype(I32)

    idx4 = jnp.zeros((tm, TOP_K), I32)
    gate4 = jnp.zeros((tm, TOP_K), F32)
    rank4 = jnp.zeros((tm, TOP_K), I32)
    for k in range(TOP_K):
        rk = jnp.sum(jnp.where(picks[k], pos, 0.0), axis=-1, keepdims=True).astype(I32)
        idx4 = jnp.where(lane_k == k, top_idx[k], idx4)
        gate4 = jnp.where(lane_k == k, exps[k] / denom, gate4)
        rank4 = jnp.where(lane_k == k, rk, rank4)
    idx_ref[...] = idx4
    gate_ref[...] = gate4
    rank_ref[...] = rank4


def _outproj(x2, oa, ob, ga, gb, w_proj_a, w_proj_b, w_out, norm_ffn_g, w_router, b_router):
    n = x2.shape[0]
    tm = ROW_TILE
    wpa = w_proj_a.astype(BF16).reshape(N_HEADS, HEAD_DIM, D_MODEL)
    wpb = w_proj_b.astype(BF16).reshape(N_HEADS, HEAD_DIM, D_MODEL)
    wr_hi = w_router.astype(BF16)
    wr_lo = (w_router - wr_hi.astype(F32)).astype(BF16)
    full = lambda shape: pl.BlockSpec(shape, lambda i: (0,) * len(shape))
    row = lambda width: pl.BlockSpec((tm, width), lambda i: (i, 0))
    head_spec = pl.BlockSpec((N_HEADS, tm, HEAD_DIM), lambda i: (0, i, 0))
    return pl.pallas_call(
        _outproj_kernel,
        grid=(n // tm,),
        in_specs=[row(D_MODEL), head_spec, head_spec, row(D_MODEL), row(D_MODEL),
                  full((N_HEADS, HEAD_DIM, D_MODEL)), full((N_HEADS, HEAD_DIM, D_MODEL)),
                  full((D_MODEL, D_MODEL)), full((1, D_MODEL)),
                  full((D_MODEL, N_EXPERTS)), full((D_MODEL, N_EXPERTS)), full((1, N_EXPERTS))],
        out_specs=[row(D_MODEL), row(D_MODEL), row(TOP_K), row(TOP_K), row(TOP_K),
                   full((1, N_EXPERTS))],
        out_shape=[jax.ShapeDtypeStruct((n, D_MODEL), F32), jax.ShapeDtypeStruct((n, D_MODEL), F32),
                   jax.ShapeDtypeStruct((n, TOP_K), I32), jax.ShapeDtypeStruct((n, TOP_K), F32),
                   jax.ShapeDtypeStruct((n, TOP_K), I32), jax.ShapeDtypeStruct((1, N_EXPERTS), I32)],
        scratch_shapes=[pltpu.VMEM((1, N_EXPERTS), F32)],
        compiler_params=pltpu.CompilerParams(vmem_limit_bytes=VMEM_LIMIT,
                                             dimension_semantics=("arbitrary",)),
        name="outproj_router",
    )(x2, oa, ob, ga, gb, wpa, wpb, w_out.astype(BF16), norm_ffn_g[None, :],
      wr_hi, wr_lo, b_router[None, :])


def _dispatch_kernel(dest_ref, fill_lo_ref, fill_hi_ref, hn_ref, xs_ref, zero_ref, sem, zsem):
    s = pl.program_id(0)
    dt = hn_ref.shape[0]

    def tok(tt, carry):
        t = s * dt + tt
        for k in range(TOP_K):
            d = dest_ref[t * TOP_K + k]
            pltpu.make_async_copy(hn_ref.at[pl.ds(tt, 1)], xs_ref.at[pl.ds(d, 1)],
                                  sem).start(priority=k % 2)
        return carry
    lax.fori_loop(0, dt, tok, 0, unroll=4)

    @pl.when(s == 0)
    def _():
        zero_ref[...] = jnp.zeros_like(zero_ref)

        def expert(e, total):
            lo, hi = fill_lo_ref[e], fill_hi_ref[e]

            def fill(r, carry):
                pltpu.make_async_copy(zero_ref, xs_ref.at[pl.ds(r, 1)], zsem).start()
                return carry
            lax.fori_loop(lo, hi, fill, 0)
            return total + (hi - lo)
        total = lax.fori_loop(0, fill_lo_ref.shape[0], expert, jnp.int32(0))

        def drain(r, carry):
            pltpu.make_async_copy(zero_ref, xs_ref.at[pl.ds(0, 1)], zsem).wait()
            return carry
        lax.fori_loop(0, total, drain, 0)

    for _ in range(TOP_K):
        pltpu.make_async_copy(hn_ref, xs_ref.at[pl.ds(0, dt)], sem).wait()


def _dispatch(dest_flat, fill_lo, fill_hi, hn, rows):
    n = hn.shape[0]
    dt = DISPATCH_T
    return pl.pallas_call(
        _dispatch_kernel,
        grid_spec=pltpu.PrefetchScalarGridSpec(
            num_scalar_prefetch=3,
            grid=(n // dt,),
            in_specs=[pl.BlockSpec((dt, D_MODEL), lambda i, d, lo, hi: (i, 0))],
            out_specs=pl.BlockSpec(memory_space=pl.ANY),
            scratch_shapes=[pltpu.VMEM((1, D_MODEL), F32),
                            pltpu.SemaphoreType.DMA(()), pltpu.SemaphoreType.DMA(())]),
        out_shape=jax.ShapeDtypeStruct((rows, D_MODEL), F32),
        compiler_params=pltpu.CompilerParams(vmem_limit_bytes=VMEM_LIMIT,
                                             dimension_semantics=("arbitrary",),
                                             has_side_effects=True),
        name="dispatch",
    )(dest_flat, fill_lo, fill_hi, hn)


def _expert_kernel(be_ref, nu_ref, x_ref, wgu_ref, bgu_ref, wd_ref, bd_ref, y_ref,
                   wgu_bf, wd_bf):
    b = pl.program_id(0)
    n_used = nu_ref[0]
    prev = be_ref[jnp.maximum(b - 1, 0)]

    @pl.when((b < n_used) & ((b == 0) | (be_ref[b] != prev)))
    def _():
        wgu_bf[...] = wgu_ref[0].astype(BF16)
        wd_bf[...] = wd_ref[0].astype(BF16)

    @pl.when(b < n_used)
    def _():
        xb = x_ref[...].astype(BF16)
        hgu = jnp.dot(xb, wgu_bf[...], preferred_element_type=F32) + bgu_ref[0]
        gate = jnp.minimum(hgu[:, :D_EXPERT], SWIGLU_LIMIT)
        up = jnp.clip(hgu[:, D_EXPERT:], -SWIGLU_LIMIT, SWIGLU_LIMIT)
        act = (up + 1.0) * gate * jax.nn.sigmoid(SWIGLU_ALPHA * gate)
        y_ref[...] = (jnp.dot(act.astype(BF16), wd_bf[...], preferred_element_type=F32)
                      + bd_ref[0])

    @pl.when(b >= n_used)
    def _():
        y_ref[...] = jnp.zeros_like(y_ref)


def _experts(block_expert, n_used, xs, w_gu, b_gu, w_down, b_down):
    rows = xs.shape[0]
    n_blocks = rows // EXPERT_BLOCK

    def blk(b, be, nu):
        return jnp.minimum(b, nu[0] - 1)

    w_spec = lambda d1, d2: pl.BlockSpec((1, d1, d2), lambda b, be, nu: (be[blk(b, be, nu)], 0, 0))
    return pl.pallas_call(
        _expert_kernel,
        grid_spec=pltpu.PrefetchScalarGridSpec(
            num_scalar_prefetch=2,
            grid=(n_blocks,),
            in_specs=[pl.BlockSpec((EXPERT_BLOCK, D_MODEL), lambda b, be, nu: (blk(b, be, nu), 0)),
                      w_spec(D_MODEL, 2 * D_EXPERT), w_spec(1, 2 * D_EXPERT),
                      w_spec(D_EXPERT, D_MODEL), w_spec(1, D_MODEL)],
            out_specs=pl.BlockSpec((EXPERT_BLOCK, D_MODEL), lambda b, be, nu: (b, 0)),
            scratch_shapes=[pltpu.VMEM((D_MODEL, 2 * D_EXPERT), BF16),
                            pltpu.VMEM((D_EXPERT, D_MODEL), BF16)]),
        out_shape=jax.ShapeDtypeStruct((rows, D_MODEL), F32),
        compiler_params=pltpu.CompilerParams(vmem_limit_bytes=VMEM_LIMIT,
                                             dimension_semantics=("arbitrary",)),
        name="experts",
    )(block_expert, n_used, xs, w_gu, b_gu[:, None, :], w_down, b_down[:, None, :])


def _combine_kernel(dest_ref, h_ref, gate_ref, y_ref, o_ref, buf_ref, sems):
    s = pl.program_id(0)
    n_steps = pl.num_programs(0)
    ct = COMBINE_T

    def issue(step, slot):
        def tok(tt, carry):
            t = step * ct + tt
            for k in range(TOP_K):
                d = dest_ref[t * TOP_K + k]
                pltpu.make_async_copy(y_ref.at[pl.ds(d, 1)], buf_ref.at[slot, k, pl.ds(tt, 1)],
                                      sems.at[slot]).start(priority=k % 2)
            return carry
        lax.fori_loop(0, ct, tok, 0, unroll=4)

    @pl.when(s == 0)
    def _():
        issue(0, 0)

    for parity in range(2):
        @pl.when((s + 1 < n_steps) & ((s + 1) % 2 == parity))
        def _():
            issue(s + 1, parity)

    slot = s % 2
    for k in range(TOP_K):
        pltpu.make_async_copy(y_ref.at[pl.ds(0, ct)], buf_ref.at[slot, k], sems.at[slot]).wait()
    g = gate_ref[...]
    out = h_ref[...]
    for k in range(TOP_K):
        out = out + g[:, k:k + 1] * buf_ref[slot, k]
    o_ref[...] = out


def _combine(dest_flat, h, gates, y):
    n = h.shape[0]
    ct = COMBINE_T
    return pl.pallas_call(
        _combine_kernel,
        grid_spec=pltpu.PrefetchScalarGridSpec(
            num_scalar_prefetch=1,
            grid=(n // ct,),
            in_specs=[pl.BlockSpec((ct, D_MODEL), lambda i, d: (i, 0)),
                      pl.BlockSpec((ct, TOP_K), lambda i, d: (i, 0)),
                      pl.BlockSpec(memory_space=pl.ANY)],
            out_specs=pl.BlockSpec((ct, D_MODEL), lambda i, d: (i, 0)),
            scratch_shapes=[pltpu.VMEM((2, TOP_K, ct, D_MODEL), F32),
                            pltpu.SemaphoreType.DMA((2,))]),
        out_shape=jax.ShapeDtypeStruct((n, D_MODEL), F32),
        compiler_params=pltpu.CompilerParams(vmem_limit_bytes=VMEM_LIMIT,
                                             dimension_semantics=("arbitrary",)),
        name="combine",
    )(dest_flat, h, gates, y)


def _moe(h, hn, top_idx, gates, rank, counts, w_gate_up, b_gate_up, w_down, b_down):
    n = h.shape[0]
    m = n * TOP_K
    n_blocks = (m + N_EXPERTS * (EXPERT_BLOCK - 1) + EXPERT_BLOCK - 1) // EXPERT_BLOCK
    rows = n_blocks * EXPERT_BLOCK
    counts = counts[0]
    padded = ((counts + EXPERT_BLOCK - 1) // EXPERT_BLOCK) * EXPERT_BLOCK
    pend = jnp.cumsum(padded).astype(I32)
    pstart = pend - padded
    dest = (pstart[top_idx] + rank).reshape(m).astype(I32)
    block_start = jnp.arange(n_blocks, dtype=I32) * EXPERT_BLOCK
    block_expert = jnp.minimum(jnp.sum(pend[None, :] <= block_start[:, None], axis=1),
                               N_EXPERTS - 1).astype(I32)
    n_used = (pend[-1:] // EXPERT_BLOCK).astype(I32)
    fill_lo = jnp.concatenate([pstart + counts, pend[-1:]]).astype(I32)
    fill_hi = jnp.concatenate([pend, jnp.full((1,), rows, I32)])
    xs = _dispatch(dest, fill_lo, fill_hi, hn, rows)
    y = _experts(block_expert, n_used, xs, w_gate_up, b_gate_up, w_down, b_down)
    return _combine(dest, h, gates, y)


def kernel(x, norm_mix_g, w_in, q_norm_g, k_norm_g, w_proj_a, w_proj_b, w_out, norm_ffn_g,
           w_router, b_router, w_gate_up, b_gate_up, w_down, b_down):
    batch, seq, d = x.shape
    h = x.reshape(batch * seq, d)
    for l in range(norm_mix_g.shape[0]):
        (qa, qi, qb, kb, vb, ka, ki, va, wi, ga, gb) = _inproj(
            h, norm_mix_g[l], w_in[l], q_norm_g[l], k_norm_g[l], seq)
        oa = _dsa(qa, qi, wi, ka, ki, va, batch, seq)
        ob = _sb(qb, kb, vb, batch, seq)
        h_mid, hn, top_idx, gates, rank, counts = _outproj(
            h, oa, ob, ga, gb, w_proj_a[l], w_proj_b[l], w_out[l], norm_ffn_g[l],
            w_router[l], b_router[l])
        h = _moe(h_mid, hn, top_idx, gates, rank, counts,
                 w_gate_up[l], b_gate_up[l], w_down[l], b_down[l])
    return h.reshape(batch, seq, d)
```

```python
import functools

import jax
import jax.numpy as jnp
from jax import lax
from jax.experimental import pallas as pl
from jax.experimental.pallas import tpu as pltpu

F32 = jnp.float32
BF16 = jnp.bfloat16
I32 = jnp.int32

D_MODEL = 1024
CHUNK = 64
HEAD_DIM = 64
N_HEADS = 8
MAX_SELECTED_KEYS = 256
ROT_DIM = HEAD_DIM // 4
ROPE_THETA = 500000.0
N_EXPERTS = 32
TOP_K = 4
D_EXPERT = D_MODEL
SWIGLU_LIMIT = 7.0
SWIGLU_ALPHA = 1.702
EXPERT_BLOCK = 256
RMS_EPS = 1e-6

LANES = 128
VMEM_LIMIT = 56 * 1024 * 1024

ROW_TILE = 256
DSA_QB = 128
DSA_KC = 512
SB_T = 256
SB_HEADS = 8
COMBINE_T = 128
DISPATCH_T = 512

NEG_BIG = -1e30
TINY = 1e-30
SB_LOG_ZERO = -120.0


def _nt_dot(a, b):
    return lax.dot_general(a, b, (((1,), (1,)), ((), ())), preferred_element_type=F32)


def _split_bf16(x):
    hi = x.astype(BF16)
    lo = (x - hi.astype(F32)).astype(BF16)
    return hi, lo


_OFF_QA, _OFF_QI, _OFF_QB, _OFF_KB, _OFF_VB = 0, 512, 1024, 1536, 2048
_OFF_KK, _OFF_VW, _OFF_GA, _OFF_GB, _W_COLS = 2560, 2688, 2816, 3840, 4864


def _rope128(y, c, s1, s2):
    return y * c + pltpu.roll(y, 8, 1) * s1 + pltpu.roll(y, LANES - 8, 1) * s2


def _head_rms_inv(x, norm_lo, norm_hi):
    lane = lax.broadcasted_iota(I32, x.shape, 1)
    lo = lane < HEAD_DIM
    sq = x * x
    one = jnp.ones((x.shape[0], 1), F32)
    inv_lo = one
    inv_hi = one
    if norm_lo:
        s = jnp.sum(jnp.where(lo, sq, 0.0), axis=-1, keepdims=True)
        inv_lo = lax.rsqrt(s * (1.0 / HEAD_DIM) + RMS_EPS)
    if norm_hi:
        s = jnp.sum(jnp.where(lo, 0.0, sq), axis=-1, keepdims=True)
        inv_hi = lax.rsqrt(s * (1.0 / HEAD_DIM) + RMS_EPS)
    return jnp.where(lo, inv_lo, inv_hi)


def _inproj_kernel(x_ref, gmix_ref, w_ref, wvt_ref, qg_ref, kg_ref, c_ref, s1_ref, s2_ref,
                   qa_ref, qi_ref, qb_ref, kb_ref, vb_ref, ka_ref, ki_ref, va_ref,
                   wi_ref, ga_ref, gb_ref):
    x = x_ref[...]
    ms = jnp.mean(x * x, axis=-1, keepdims=True)
    xn = (x * lax.rsqrt(ms + RMS_EPS) * gmix_ref[...]).astype(BF16)

    def proj(off, width):
        return jnp.dot(xn, w_ref[:, off:off + width], preferred_element_type=F32)

    c, s1, s2 = c_ref[...], s1_ref[...], s2_ref[...]
    scale = HEAD_DIM ** -0.5

    def put_heads(dst_ref, pair, val):
        dst_ref[2 * pair] = val[:, :HEAD_DIM].astype(BF16)
        dst_ref[2 * pair + 1] = val[:, HEAD_DIM:].astype(BF16)

    acc = proj(_OFF_QA, 512)
    for p in range(4):
        blk = acc[:, p * LANES:(p + 1) * LANES]
        y = blk * _head_rms_inv(blk, True, True) * qg_ref[...]
        put_heads(qa_ref, p, _rope128(y, c, s1, s2) * scale)

    acc = proj(_OFF_QI, 512)
    for p in range(4):
        blk = acc[:, p * LANES:(p + 1) * LANES]
        put_heads(qi_ref, p, _rope128(blk, c, s1, s2) * scale)

    acc = proj(_OFF_QB, 512)
    for p in range(4):
        put_heads(qb_ref, p, acc[:, p * LANES:(p + 1) * LANES] * scale)
    acc = proj(_OFF_KB, 512)
    for p in range(4):
        put_heads(kb_ref, p, acc[:, p * LANES:(p + 1) * LANES])
    acc = proj(_OFF_VB, 512)
    for p in range(4):
        put_heads(vb_ref, p, acc[:, p * LANES:(p + 1) * LANES])

    kk = proj(_OFF_KK, LANES)
    y = kk * _head_rms_inv(kk, True, False) * kg_ref[...]
    y = _rope128(y, c, s1, s2)
    ka_ref[...] = y[:, :HEAD_DIM].astype(BF16)
    ki_ref[...] = y[:, HEAD_DIM:].astype(BF16)

    vw_t = _nt_dot(wvt_ref[...], xn)
    va_ref[0] = vw_t[:HEAD_DIM].astype(BF16)
    wi_ref[...] = vw_t[HEAD_DIM:HEAD_DIM + N_HEADS]

    ga_ref[...] = proj(_OFF_GA, D_MODEL)
    gb_ref[...] = proj(_OFF_GB, D_MODEL)


def _rope_tables(seq):
    pos = jnp.arange(seq, dtype=F32)
    inv_freq = ROPE_THETA ** (-jnp.arange(0, ROT_DIM, 2, dtype=F32) / ROT_DIM)
    ang = pos[:, None] * inv_freq[None, :]
    cos, sin = jnp.cos(ang), jnp.sin(ang)
    half = ROT_DIM // 2
    zeros = jnp.zeros((seq, HEAD_DIM - ROT_DIM), F32)
    zh = jnp.zeros((seq, half), F32)
    c64 = jnp.concatenate([cos, cos, zeros + 1.0], axis=1)
    s1_64 = jnp.concatenate([zh, sin, zeros], axis=1)
    s2_64 = jnp.concatenate([-sin, zh, zeros], axis=1)
    tile2 = lambda t: jnp.concatenate([t, t], axis=1)
    return tile2(c64), tile2(s1_64), tile2(s2_64)


def _inproj(x2, norm_g, w_in, q_norm_g, k_norm_g, seq):
    n = x2.shape[0]
    w = w_in
    sl = lambda a, b: w[:, a:b]
    o = [0, 512, 576, 640, 1152, 1216, 1224, 1736, 2248, 2760, 3784, 4808]
    pad = jnp.zeros((D_MODEL, LANES - HEAD_DIM - N_HEADS), w.dtype)
    w_all = jnp.concatenate([
        sl(o[0], o[1]), sl(o[3], o[4]), sl(o[6], o[7]), sl(o[7], o[8]), sl(o[8], o[9]),
        sl(o[1], o[2]), sl(o[4], o[5]),
        sl(o[2], o[3]), sl(o[5], o[6]), pad,
        sl(o[9], o[10]), sl(o[10], o[11])], axis=1).astype(BF16)
    assert w_all.shape[1] == _W_COLS
    qg = jnp.concatenate([q_norm_g, q_norm_g])[None, :]
    kg = jnp.concatenate([k_norm_g, jnp.ones_like(k_norm_g)])[None, :]
    c, s1, s2 = _rope_tables(seq)
    tm = ROW_TILE
    pos_blocks = seq // tm
    full = lambda shape: pl.BlockSpec(shape, lambda i: (0,) * len(shape))
    head_out = jax.ShapeDtypeStruct((N_HEADS, n, HEAD_DIM), BF16)
    head_spec = pl.BlockSpec((N_HEADS, tm, HEAD_DIM), lambda i: (0, i, 0))
    row64 = jax.ShapeDtypeStruct((n, HEAD_DIM), BF16)
    row64_spec = pl.BlockSpec((tm, HEAD_DIM), lambda i: (i, 0))
    tab_spec = pl.BlockSpec((tm, LANES), lambda i: (i % pos_blocks, 0))
    return pl.pallas_call(
        _inproj_kernel,
        grid=(n // tm,),
        in_specs=[pl.BlockSpec((tm, D_MODEL), lambda i: (i, 0)),
                  full((1, D_MODEL)), full((D_MODEL, _W_COLS)), full((LANES, D_MODEL)),
                  full((1, LANES)), full((1, LANES)), tab_spec, tab_spec, tab_spec],
        out_specs=[head_spec] * 5 + [row64_spec] * 2
                  + [pl.BlockSpec((1, HEAD_DIM, tm), lambda i: (i, 0, 0)),
                     pl.BlockSpec((N_HEADS, tm), lambda i: (0, i)),
                     pl.BlockSpec((tm, D_MODEL), lambda i: (i, 0)),
                     pl.BlockSpec((tm, D_MODEL), lambda i: (i, 0))],
        out_shape=[head_out] * 5 + [row64] * 2
                  + [jax.ShapeDtypeStruct((n // tm, HEAD_DIM, tm), BF16),
                     jax.ShapeDtypeStruct((N_HEADS, n), F32),
                     jax.ShapeDtypeStruct((n, D_MODEL), F32),
                     jax.ShapeDtypeStruct((n, D_MODEL), F32)],
        compiler_params=pltpu.CompilerParams(vmem_limit_bytes=VMEM_LIMIT),
        name="inproj",
    )(x2, norm_g[None, :], w_all, w_all[:, _OFF_VW:_OFF_VW + LANES].T, qg, kg, c, s1, s2)


def _key_to_float(u):
    k = u ^ jnp.int32(-2 ** 31)
    b = k ^ ((k >> 31) & jnp.int32(0x7FFFFFFF))
    return lax.bitcast_convert_type(b, F32)


def _float_to_key(x):
    b = lax.bitcast_convert_type(x, I32)
    return b ^ ((b >> 31) & jnp.int32(0x7FFFFFFF)) ^ jnp.int32(-2 ** 31)


def _bit_transpose32(words):
    a = list(words)
    j, m = 16, 0x0000FFFF
    while j:
        k = 0
        while k < 32:
            t = (a[k] ^ lax.shift_right_logical(a[k + j], jnp.int32(j))) & jnp.int32(_as_i32(m))
            a[k] = a[k] ^ t
            a[k + j] = a[k + j] ^ (t << j)
            k = (k + j + 1) & ~j
        j >>= 1
        m = (m ^ (m << j)) & 0xFFFFFFFF
    return a[::-1]


def _as_i32(v):
    return v - (1 << 32) if v >= (1 << 31) else v


def _slab_reduce(x, op):
    parts = [x[j * 8:(j + 1) * 8] for j in range(x.shape[0] // 8)]
    while len(parts) > 1:
        nxt = [op(parts[j], parts[j + 1]) for j in range(0, len(parts) - 1, 2)]
        if len(parts) % 2:
            nxt.append(parts[-1])
        parts = nxt
    return parts[0]


def _dsa_kernel(n_sel, qa_ref, qi_ref, wi_ref, ka_ref, ki_ref, va_ref, o_ref,
                score_ref, planes_ref, alive_ref, m_ref, l_ref, acc_ref):
    i = pl.program_id(1)
    qb, kc = DSA_QB, DSA_KC
    pairs = range(N_HEADS // 2)
    va_pieces = kc // ROW_TILE
    n_chunks = (i * qb + qb + kc - 1) // kc
    t_pos = i * qb + lax.broadcasted_iota(I32, (1, qb), 1)
    limit = (t_pos // CHUNK + 1) * CHUNK
    kio = lax.broadcasted_iota(I32, (kc, qb), 0)

    def pair_rows(ref, p):
        return ref[2 * p:2 * p + 2].reshape(2 * qb, HEAD_DIM)

    def twice(x):
        return jnp.concatenate([x, x], axis=1)

    w = wi_ref[...] * (N_HEADS ** -0.5)
    w2 = [jnp.concatenate([w[2 * p:2 * p + 1], w[2 * p + 1:2 * p + 2]], axis=1) for p in pairs]
    q_idx = [pair_rows(qi_ref, p) for p in pairs]
    groups_per_chunk = kc // (8 * 32)
    n_groups = planes_ref.shape[1]

    def idx_body(c, carry):
        start = pl.multiple_of(c * kc, kc)
        keys = ki_ref[pl.ds(start, kc), :]
        tot = None
        for p in pairs:
            t = jnp.maximum(_nt_dot(keys, q_idx[p]), 0.0) * w2[p]
            tot = t if tot is None else tot + t
        sc = jnp.where(start + kio < limit, tot[:, :qb] + tot[:, qb:], -jnp.inf)
        score_ref[c] = sc
        okey = _float_to_key(sc)
        for g in range(groups_per_chunk):
            slabs = [okey[(g * 32 + j) * 8:(g * 32 + j + 1) * 8] for j in range(32)]
            planes = _bit_transpose32(slabs)
            for bit in range(32):
                planes_ref[bit, groups_per_chunk * c + g] = planes[bit]
        return carry

    lax.fori_loop(0, n_chunks, idx_body, 0)

    def count(pred_fn):
        def body(c, acc):
            return acc + _slab_reduce(pred_fn(c).astype(F32), jnp.add)
        acc = lax.fori_loop(0, n_chunks, body, jnp.zeros((8, qb), F32))
        return jnp.sum(acc, axis=0, keepdims=True)

    def kth_largest(val_fn):
        def body(it, carry):
            u, n_ge = carry
            trial = u | (jnp.int32(1) << (31 - it))
            cand = _key_to_float(trial)
            cnt = count(lambda c: val_fn(c) >= cand)
            take = cnt >= n_sel
            return jnp.where(take, trial, u), jnp.where(take, cnt, n_ge)
        u, n_ge = lax.fori_loop(0, 32, body,
                                (jnp.zeros((1, qb), I32), jnp.full((1, qb), n_sel, F32)))
        return _key_to_float(u), n_ge

    def kth_largest_key():
        used = groups_per_chunk * n_chunks
        for g in range(n_groups):
            @pl.when(g >= used)
            def _():
                for bit in range(32):
                    planes_ref[bit, g] = jnp.zeros((8, qb), I32)
            alive_ref[g] = jnp.where(g < used, jnp.int32(-1), jnp.int32(0)) + jnp.zeros((8, qb), I32)

        def body(it, carry):
            u, want = carry
            bit = 31 - it
            cnt8 = jnp.zeros((8, qb), I32)
            for g in range(n_groups):
                cnt8 = cnt8 + lax.population_count(alive_ref[g] & planes_ref[bit, g])
            cnt = jnp.sum(cnt8, axis=0, keepdims=True)
            take = cnt >= want
            flip = jnp.where(take, jnp.int32(0), jnp.int32(-1))
            for g in range(n_groups):
                alive_ref[g] = alive_ref[g] & (planes_ref[bit, g] ^ flip)
            return (jnp.where(take, u | (jnp.int32(1) << bit), u),
                    jnp.where(take, want, want - cnt))
        u, _ = lax.fori_loop(0, 32, body,
                             (jnp.zeros((1, qb), I32), jnp.full((1, qb), n_sel, I32)))
        return _key_to_float(u)

    select_all = limit <= n_sel
    thr = kth_largest_key()
    n_ge = count(lambda c: score_ref[c] >= thr)
    n_gt = count(lambda c: score_ref[c] > thr)
    settled = select_all | ((n_gt < n_sel) & (n_ge >= n_sel))
    thr, n_ge = lax.cond(jnp.min(jnp.where(settled, 1, 0)) > 0,
                         lambda: (thr, n_ge), lambda: kth_largest(lambda c: score_ref[c]))
    thr = jnp.where(select_all, -jnp.inf, thr)

    def to_excess(c, carry):
        s = score_ref[c]
        above = jnp.where(s > thr, jnp.maximum(s - thr, TINY), 0.0)
        score_ref[c] = jnp.where(s >= thr, above, -jnp.inf)
        return carry
    lax.fori_loop(0, n_chunks, to_excess, 0)

    def refine():
        n_above = count(lambda c: score_ref[c] > 0.0)
        crowded = jnp.max(jnp.where((n_above >= n_sel) & jnp.logical_not(select_all), 1, 0)) > 0

        def split():
            t2, _ = kth_largest(lambda c: score_ref[c])
            return t2, n_sel - count(lambda c: score_ref[c] > t2)
        thr2, need = lax.cond(crowded, split, lambda: (jnp.zeros((1, qb), F32), n_sel - n_above))

        def tie_body(it, j):
            trial = j | (jnp.int32(1) << (12 - it))
            g = count(lambda c: (score_ref[c] == thr2) & (c * kc + kio < trial))
            return jnp.where(g <= need, trial, j)
        return thr2, lax.fori_loop(0, 13, tie_body, jnp.zeros((1, qb), I32))

    tied = jnp.max(jnp.where((n_ge > n_sel) & jnp.logical_not(select_all), 1, 0)) > 0
    thr2, j_lim = lax.cond(tied, refine, lambda: (jnp.zeros((1, qb), F32),
                                                  jnp.full((1, qb), 2 ** 30, I32)))
    thr2 = jnp.where(select_all, 0.0, thr2)
    j_lim = jnp.where(select_all, jnp.int32(2 ** 30), j_lim)

    q_att = [pair_rows(qa_ref, p) for p in pairs]
    m_ref[...] = jnp.full_like(m_ref, NEG_BIG)
    l_ref[...] = jnp.zeros_like(l_ref)
    acc_ref[...] = jnp.zeros_like(acc_ref)

    def att_body(c, carry):
        start = pl.multiple_of(c * kc, kc)
        kpos = start + kio
        ex = score_ref[c]
        sel = (kpos < limit) & ((ex > thr2) | ((ex == thr2) & (kpos < j_lim)))
        bias = twice(jnp.where(sel, 0.0, NEG_BIG))
        keys = ka_ref[pl.ds(start, kc), :]
        s = [_nt_dot(keys, q_att[p]) + bias for p in pairs]
        m_old = [m_ref[p] for p in pairs]
        m_new = [jnp.maximum(m_old[p], jnp.max(_slab_reduce(s[p], jnp.maximum), axis=0,
                                               keepdims=True)) for p in pairs]
        alpha = [jnp.exp(m_old[p] - m_new[p]) for p in pairs]
        pe = [jnp.exp(s[p] - m_new[p]) for p in pairs]
        pv = []
        for p in pairs:
            pb = pe[p].astype(BF16)
            tot = None
            for v in range(va_pieces):
                part = jnp.dot(va_ref[va_pieces * c + v], pb[v * ROW_TILE:(v + 1) * ROW_TILE],
                               preferred_element_type=F32)
                tot = part if tot is None else tot + part
            pv.append(tot)
        for p in pairs:
            l_ref[p] = alpha[p] * l_ref[p] + jnp.sum(_slab_reduce(pe[p], jnp.add), axis=0,
                                                     keepdims=True)
            acc_ref[p] = acc_ref[p] * alpha[p] + pv[p]
            m_ref[p] = m_new[p]
        return carry

    lax.fori_loop(0, n_chunks, att_body, 0)
    for p in pairs:
        out = acc_ref[p] / l_ref[p]
        o_ref[2 * p] = out[:, :qb].T.astype(BF16)
        o_ref[2 * p + 1] = out[:, qb:].T.astype(BF16)


def _dsa(qa, qi, wi, ka, ki, va, batch, seq):
    n = batch * seq
    nqb = seq // DSA_QB
    n_sel = min(MAX_SELECTED_KEYS, seq // 4)
    head_spec = pl.BlockSpec((N_HEADS, DSA_QB, HEAD_DIM), lambda b, i: (0, b * nqb + i, 0))
    kv_spec = pl.BlockSpec((seq, HEAD_DIM), lambda b, i: (b, 0))
    pair_stat = pltpu.VMEM((N_HEADS // 2, 1, 2 * DSA_QB), F32)
    return pl.pallas_call(
        functools.partial(_dsa_kernel, n_sel),
        grid=(batch, nqb),
        in_specs=[head_spec, head_spec,
                  pl.BlockSpec((N_HEADS, DSA_QB), lambda b, i: (0, b * nqb + i)),
                  kv_spec, kv_spec,
                  pl.BlockSpec((seq // ROW_TILE, HEAD_DIM, ROW_TILE), lambda b, i: (b, 0, 0))],
        out_specs=head_spec,
        out_shape=jax.ShapeDtypeStruct((N_HEADS, n, HEAD_DIM), BF16),
        scratch_shapes=[pltpu.VMEM((seq // DSA_KC, DSA_KC, DSA_QB), F32),
                        pltpu.VMEM((32, seq // 256, 8, DSA_QB), I32),
                        pltpu.VMEM((seq // 256, 8, DSA_QB), I32), pair_stat, pair_stat,
                        pltpu.VMEM((N_HEADS // 2, HEAD_DIM, 2 * DSA_QB), F32)],
        compiler_params=pltpu.CompilerParams(vmem_limit_bytes=VMEM_LIMIT),
        name="dsa",
    )(qa, qi, wi, ka, ki, va)


def _sb_kernel(q_ref, k_ref, v_ref, o_ref):
    i = pl.program_id(2)
    t = SB_T
    row = lax.broadcasted_iota(I32, (t, t), 0)
    col = lax.broadcasted_iota(I32, (t, t), 1)
    tri = jnp.where(row > col, 1.0, 0.0).astype(BF16)
    causal = col < row

    def block(j, state, diag):
        heads = range(SB_HEADS)
        start = pl.multiple_of(j * t, t)
        zs = [_nt_dot(q_ref[hd], k_ref[hd, pl.ds(start, t), :]) for hd in heads]
        lks = []
        for z in zs:
            lk = -(jnp.maximum(z, 0.0) + jnp.log(1.0 + jnp.exp(-jnp.abs(z))))
            lks.append(jnp.where(causal, lk, 0.0) if diag else lk)
        parts = [_split_bf16(lk) for lk in lks]
        laters = [jnp.dot(hi, tri, preferred_element_type=F32)
                  + jnp.dot(lo, tri, preferred_element_type=F32)
                  for hi, lo in parts]
        out = []
        for hd in heads:
            carry, acc = state[hd]
            a = jnp.exp(zs[hd] + lks[hd] + laters[hd] + carry)
            if diag:
                a = jnp.where(causal, a, 0.0)
            acc = acc + jnp.dot(a.astype(BF16), v_ref[hd, pl.ds(start, t), :],
                                preferred_element_type=F32)
            out.append((carry + jnp.sum(lks[hd], axis=-1, keepdims=True), acc))
        return tuple(out)

    state = tuple((jnp.zeros((t, 1), F32), jnp.zeros((t, HEAD_DIM), F32))
                  for _ in range(SB_HEADS))
    state = block(i, state, True)

    def live(loop):
        jj, st = loop
        top = st[0][0]
        for hd in range(1, SB_HEADS):
            top = jnp.maximum(top, st[hd][0])
        return (jj < i) & (jnp.max(top) > SB_LOG_ZERO)

    _, state = lax.while_loop(live, lambda loop: (loop[0] + 1, block(i - 1 - loop[0], loop[1], False)),
                              (jnp.int32(0), state))
    for hd in range(SB_HEADS):
        o_ref[hd] = state[hd][1].astype(BF16)


def _sb(qb, kb, vb, batch, seq):
    n = batch * seq
    nq = seq // SB_T
    q_spec = pl.BlockSpec((SB_HEADS, SB_T, HEAD_DIM), lambda b, h, i: (h, b * nq + i, 0))
    kv_spec = pl.BlockSpec((SB_HEADS, seq, HEAD_DIM), lambda b, h, i: (h, b, 0))
    return pl.pallas_call(
        _sb_kernel,
        grid=(batch, N_HEADS // SB_HEADS, nq),
        in_specs=[q_spec, kv_spec, kv_spec],
        out_specs=q_spec,
        out_shape=jax.ShapeDtypeStruct((N_HEADS, n, HEAD_DIM), BF16),
        compiler_params=pltpu.CompilerParams(vmem_limit_bytes=VMEM_LIMIT),
        name="stickbreak",
    )(qb, kb, vb)


def _outproj_kernel(x_ref, oa_ref, ob_ref, ga_ref, gb_ref, wpa_ref, wpb_ref, wo_ref,
                    gffn_ref, wr_hi_ref, wr_lo_ref, br_ref,
                    h_ref, hn_ref, idx_ref, gate_ref, rank_ref, cnt_ref, carry_ref):
    step = pl.program_id(0)
    tm = x_ref.shape[0]

    @pl.when(step == 0)
    def _():
        carry_ref[...] = jnp.zeros_like(carry_ref)

    pa = jnp.dot(oa_ref[0], wpa_ref[0], preferred_element_type=F32)
    pb = jnp.dot(ob_ref[0], wpb_ref[0], preferred_element_type=F32)
    for hd in range(1, N_HEADS):
        pa = pa + jnp.dot(oa_ref[hd], wpa_ref[hd], preferred_element_type=F32)
        pb = pb + jnp.dot(ob_ref[hd], wpb_ref[hd], preferred_element_type=F32)
    merged = jax.nn.sigmoid(ga_ref[...]) * pa + jax.nn.sigmoid(gb_ref[...]) * pb
    h = x_ref[...] + jnp.dot(merged.astype(BF16), wo_ref[...], preferred_element_type=F32)
    h_ref[...] = h
    ms = jnp.mean(h * h, axis=-1, keepdims=True)
    hn = h * lax.rsqrt(ms + RMS_EPS) * gffn_ref[...]
    hn_ref[...] = hn

    hn_hi, hn_lo = _split_bf16(hn)
    logits = (jnp.dot(hn_hi, wr_hi_ref[...], preferred_element_type=F32)
              + jnp.dot(hn_lo, wr_hi_ref[...], preferred_element_type=F32)
              + jnp.dot(hn_hi, wr_lo_ref[...], preferred_element_type=F32)) + br_ref[...]

    lane = lax.broadcasted_iota(I32, (tm, N_EXPERTS), 1)
    lane_k = lax.broadcasted_iota(I32, (tm, TOP_K), 1)
    vals = logits
    picks, top_vals, top_idx = [], [], []
    for _ in range(TOP_K):
        mx = jnp.max(vals, axis=-1, keepdims=True)
        first = jnp.min(jnp.where(vals == mx, lane, N_EXPERTS), axis=-1, keepdims=True)
        pick = lane == first
        picks.append(pick)
        top_vals.append(mx)
        top_idx.append(first)
        vals = jnp.where(pick, -jnp.inf, vals)
    exps = [jnp.exp(v - top_vals[0]) for v in top_vals]
    denom = exps[0] + exps[1] + exps[2] + exps[3]

    any_pick = picks[0] | picks[1] | picks[2] | picks[3]
    onehot = jnp.where(any_pick, 1.0, 0.0)
    r_i = lax.broadcasted_iota(I32, (tm, tm), 0)
    c_i = lax.broadcasted_iota(I32, (tm, tm), 1)
    lower = jnp.where(c_i < r_i, 1.0, 0.0).astype(BF16)
    pos = jnp.dot(lower, onehot.astype(BF16), preferred_element_type=F32) + carry_ref[...]
    carry_ref[...] = carry_ref[...] + jnp.sum(onehot, axis=0, keepdims=True)
    cnt_ref[...] = carry_ref[...].astype(I32)

    idx4 = jnp.zeros((tm, TOP_K), I32)
    gate4 = jnp.zeros((tm, TOP_K), F32)
    rank4 = jnp.zeros((tm, TOP_K), I32)
    for k in range(TOP_K):
        rk = jnp.sum(jnp.where(picks[k], pos, 0.0), axis=-1, keepdims=True).astype(I32)
        idx4 = jnp.where(lane_k == k, top_idx[k], idx4)
        gate4 = jnp.where(lane_k == k, exps[k] / denom, gate4)
        rank4 = jnp.where(lane_k == k, rk, rank4)
    idx_ref[...] = idx4
    gate_ref[...] = gate4
    rank_ref[...] = rank4


def _outproj(x2, oa, ob, ga, gb, w_proj_a, w_proj_b, w_out, norm_ffn_g, w_router, b_router):
    n = x2.shape[0]
    tm = ROW_TILE
    wpa = w_proj_a.astype(BF16).reshape(N_HEADS, HEAD_DIM, D_MODEL)
    wpb = w_proj_b.astype(BF16).reshape(N_HEADS, HEAD_DIM, D_MODEL)
    wr_hi = w_router.astype(BF16)
    wr_lo = (w_router - wr_hi.astype(F32)).astype(BF16)
    full = lambda shape: pl.BlockSpec(shape, lambda i: (0,) * len(shape))
    row = lambda width: pl.BlockSpec((tm, width), lambda i: (i, 0))
    head_spec = pl.BlockSpec((N_HEADS, tm, HEAD_DIM), lambda i: (0, i, 0))
    return pl.pallas_call(
        _outproj_kernel,
        grid=(n // tm,),
        in_specs=[row(D_MODEL), head_spec, head_spec, row(D_MODEL), row(D_MODEL),
                  full((N_HEADS, HEAD_DIM, D_MODEL)), full((N_HEADS, HEAD_DIM, D_MODEL)),
                  full((D_MODEL, D_MODEL)), full((1, D_MODEL)),
                  full((D_MODEL, N_EXPERTS)), full((D_MODEL, N_EXPERTS)), full((1, N_EXPERTS))],
        out_specs=[row(D_MODEL), row(D_MODEL), row(TOP_K), row(TOP_K), row(TOP_K),
                   full((1, N_EXPERTS))],
        out_shape=[jax.ShapeDtypeStruct((n, D_MODEL), F32), jax.ShapeDtypeStruct((n, D_MODEL), F32),
                   jax.ShapeDtypeStruct((n, TOP_K), I32), jax.ShapeDtypeStruct((n, TOP_K), F32),
                   jax.ShapeDtypeStruct((n, TOP_K), I32), jax.ShapeDtypeStruct((1, N_EXPERTS), I32)],
        scratch_shapes=[pltpu.VMEM((1, N_EXPERTS), F32)],
        compiler_params=pltpu.CompilerParams(vmem_limit_bytes=VMEM_LIMIT,
                                             dimension_semantics=("arbitrary",)),
        name="outproj_router",
    )(x2, oa, ob, ga, gb, wpa, wpb, w_out.astype(BF16), norm_ffn_g[None, :],
      wr_hi, wr_lo, b_router[None, :])


def _dispatch_kernel(dest_ref, fill_lo_ref, fill_hi_ref, hn_ref, xs_ref, zero_ref, sem, zsem):
    s = pl.program_id(0)
    dt = hn_ref.shape[0]

    def tok(tt, carry):
        t = s * dt + tt
        for k in range(TOP_K):
            d = dest_ref[t * TOP_K + k]
            pltpu.make_async_copy(hn_ref.at[pl.ds(tt, 1)], xs_ref.at[pl.ds(d, 1)],
                                  sem).start(priority=k % 2)
        return carry
    lax.fori_loop(0, dt, tok, 0, unroll=4)

    @pl.when(s == 0)
    def _():
        zero_ref[...] = jnp.zeros_like(zero_ref)

        def expert(e, total):
            lo, hi = fill_lo_ref[e], fill_hi_ref[e]

            def fill(r, carry):
                pltpu.make_async_copy(zero_ref, xs_ref.at[pl.ds(r, 1)], zsem).start()
                return carry
            lax.fori_loop(lo, hi, fill, 0)
            return total + (hi - lo)
        total = lax.fori_loop(0, fill_lo_ref.shape[0], expert, jnp.int32(0))

        def drain(r, carry):
            pltpu.make_async_copy(zero_ref, xs_ref.at[pl.ds(0, 1)], zsem).wait()
            return carry
        lax.fori_loop(0, total, drain, 0)

    for _ in range(TOP_K):
        pltpu.make_async_copy(hn_ref, xs_ref.at[pl.ds(0, dt)], sem).wait()


def _dispatch(dest_flat, fill_lo, fill_hi, hn, rows):
    n = hn.shape[0]
    dt = DISPATCH_T
    return pl.pallas_call(
        _dispatch_kernel,
        grid_spec=pltpu.PrefetchScalarGridSpec(
            num_scalar_prefetch=3,
            grid=(n // dt,),
            in_specs=[pl.BlockSpec((dt, D_MODEL), lambda i, d, lo, hi: (i, 0))],
            out_specs=pl.BlockSpec(memory_space=pl.ANY),
            scratch_shapes=[pltpu.VMEM((1, D_MODEL), F32),
                            pltpu.SemaphoreType.DMA(()), pltpu.SemaphoreType.DMA(())]),
        out_shape=jax.ShapeDtypeStruct((rows, D_MODEL), F32),
        compiler_params=pltpu.CompilerParams(vmem_limit_bytes=VMEM_LIMIT,
                                             dimension_semantics=("arbitrary",),
                                             has_side_effects=True),
        name="dispatch",
    )(dest_flat, fill_lo, fill_hi, hn)


def _expert_kernel(be_ref, nu_ref, x_ref, wgu_ref, bgu_ref, wd_ref, bd_ref, y_ref,
                   wgu_bf, wd_bf):
    b = pl.program_id(0)
    n_used = nu_ref[0]
    prev = be_ref[jnp.maximum(b - 1, 0)]

    @pl.when((b < n_used) & ((b == 0) | (be_ref[b] != prev)))
    def _():
        wgu_bf[...] = wgu_ref[0].astype(BF16)
        wd_bf[...] = wd_ref[0].astype(BF16)

    @pl.when(b < n_used)
    def _():
        xb = x_ref[...].astype(BF16)
        hgu = jnp.dot(xb, wgu_bf[...], preferred_element_type=F32) + bgu_ref[0]
        gate = jnp.minimum(hgu[:, :D_EXPERT], SWIGLU_LIMIT)
        up = jnp.clip(hgu[:, D_EXPERT:], -SWIGLU_LIMIT, SWIGLU_LIMIT)
        act = (up + 1.0) * gate * jax.nn.sigmoid(SWIGLU_ALPHA * gate)
        y_ref[...] = (jnp.dot(act.astype(BF16), wd_bf[...], preferred_element_type=F32)
                      + bd_ref[0])

    @pl.when(b >= n_used)
    def _():
        y_ref[...] = jnp.zeros_like(y_ref)


def _experts(block_expert, n_used, xs, w_gu, b_gu, w_down, b_down):
    rows = xs.shape[0]
    n_blocks = rows // EXPERT_BLOCK

    def blk(b, be, nu):
        return jnp.minimum(b, nu[0] - 1)

    w_spec = lambda d1, d2: pl.BlockSpec((1, d1, d2), lambda b, be, nu: (be[blk(b, be, nu)], 0, 0))
    return pl.pallas_call(
        _expert_kernel,
        grid_spec=pltpu.PrefetchScalarGridSpec(
            num_scalar_prefetch=2,
            grid=(n_blocks,),
            in_specs=[pl.BlockSpec((EXPERT_BLOCK, D_MODEL), lambda b, be, nu: (blk(b, be, nu), 0)),
                      w_spec(D_MODEL, 2 * D_EXPERT), w_spec(1, 2 * D_EXPERT),
                      w_spec(D_EXPERT, D_MODEL), w_spec(1, D_MODEL)],
            out_specs=pl.BlockSpec((EXPERT_BLOCK, D_MODEL), lambda b, be, nu: (b, 0)),
            scratch_shapes=[pltpu.VMEM((D_MODEL, 2 * D_EXPERT), BF16),
                            pltpu.VMEM((D_EXPERT, D_MODEL), BF16)]),
        out_shape=jax.ShapeDtypeStruct((rows, D_MODEL), F32),
        compiler_params=pltpu.CompilerParams(vmem_limit_bytes=VMEM_LIMIT,
                                             dimension_semantics=("arbitrary",)),
        name="experts",
    )(block_expert, n_used, xs, w_gu, b_gu[:, None, :], w_down, b_down[:, None, :])


def _combine_kernel(dest_ref, h_ref, gate_ref, y_ref, o_ref, buf_ref, sems):
    s = pl.program_id(0)
    n_steps = pl.num_programs(0)
    ct = COMBINE_T

    def issue(step, slot):
        def tok(tt, carry):
            t = step * ct + tt
            for k in range(TOP_K):
                d = dest_ref[t * TOP_K + k]
                pltpu.make_async_copy(y_ref.at[pl.ds(d, 1)], buf_ref.at[slot, k, pl.ds(tt, 1)],
                                      sems.at[slot]).start(priority=k % 2)
            return carry
        lax.fori_loop(0, ct, tok, 0, unroll=4)

    @pl.when(s == 0)
    def _():
        issue(0, 0)

    for parity in range(2):
        @pl.when((s + 1 < n_steps) & ((s + 1) % 2 == parity))
        def _():
            issue(s + 1, parity)

    slot = s % 2
    for k in range(TOP_K):
        pltpu.make_async_copy(y_ref.at[pl.ds(0, ct)], buf_ref.at[slot, k], sems.at[slot]).wait()
    g = gate_ref[...]
    out = h_ref[...]
    for k in range(TOP_K):
        out = out + g[:, k:k + 1] * buf_ref[slot, k]
    o_ref[...] = out


def _combine(dest_flat, h, gates, y):
    n = h.shape[0]
    ct = COMBINE_T
    return pl.pallas_call(
        _combine_kernel,
        grid_spec=pltpu.PrefetchScalarGridSpec(
            num_scalar_prefetch=1,
            grid=(n // ct,),
            in_specs=[pl.BlockSpec((ct, D_MODEL), lambda i, d: (i, 0)),
                      pl.BlockSpec((ct, TOP_K), lambda i, d: (i, 0)),
                      pl.BlockSpec(memory_space=pl.ANY)],
            out_specs=pl.BlockSpec((ct, D_MODEL), lambda i, d: (i, 0)),
            scratch_shapes=[pltpu.VMEM((2, TOP_K, ct, D_MODEL), F32),
                            pltpu.SemaphoreType.DMA((2,))]),
        out_shape=jax.ShapeDtypeStruct((n, D_MODEL), F32),
        compiler_params=pltpu.CompilerParams(vmem_limit_bytes=VMEM_LIMIT,
                                             dimension_semantics=("arbitrary",)),
        name="combine",
    )(dest_flat, h, gates, y)


def _moe(h, hn, top_idx, gates, rank, counts, w_gate_up, b_gate_up, w_down, b_down):
    n = h.shape[0]
    m = n * TOP_K
    n_blocks = (m + N_EXPERTS * (EXPERT_BLOCK - 1) + EXPERT_BLOCK - 1) // EXPERT_BLOCK
    rows = n_blocks * EXPERT_BLOCK
    counts = counts[0]
    padded = ((counts + EXPERT_BLOCK - 1) // EXPERT_BLOCK) * EXPERT_BLOCK
    pend = jnp.cumsum(padded).astype(I32)
    pstart = pend - padded
    dest = (pstart[top_idx] + rank).reshape(m).astype(I32)
    block_start = jnp.arange(n_blocks, dtype=I32) * EXPERT_BLOCK
    block_expert = jnp.minimum(jnp.sum(pend[None, :] <= block_start[:, None], axis=1),
                               N_EXPERTS - 1).astype(I32)
    n_used = (pend[-1:] // EXPERT_BLOCK).astype(I32)
    fill_lo = jnp.concatenate([pstart + counts, pend[-1:]]).astype(I32)
    fill_hi = jnp.concatenate([pend, jnp.full((1,), rows, I32)])
    xs = _dispatch(dest, fill_lo, fill_hi, hn, rows)
    y = _experts(block_expert, n_used, xs, w_gate_up, b_gate_up, w_down, b_down)
    return _combine(dest, h, gates, y)


def kernel(x, norm_mix_g, w_in, q_norm_g, k_norm_g, w_proj_a, w_proj_b, w_out, norm_ffn_g,
           w_router, b_router, w_gate_up, b_gate_up, w_down, b_down):
    batch, seq, d = x.shape
    h = x.reshape(batch * seq, d)
    for l in range(norm_mix_g.shape[0]):
        (qa, qi, qb, kb, vb, ka, ki, va, wi, ga, gb) = _inproj(
            h, norm_mix_g[l], w_in[l], q_norm_g[l], k_norm_g[l], seq)
        oa = _dsa(qa, qi, wi, ka, ki, va, batch, seq)
        ob = _sb(qb, kb, vb, batch, seq)
        h_mid, hn, top_idx, gates, rank, counts = _outproj(
            h, oa, ob, ga, gb, w_proj_a[l], w_proj_b[l], w_out[l], norm_ffn_g[l],
            w_router[l], b_router[l])
        h = _moe(h_mid, hn, top_idx, gates, rank, counts,
                 w_gate_up[l], b_gate_up[l], w_down[l], b_down[l])
    return h.reshape(batch, seq, d)
```

```python
import functools

import jax
import jax.numpy as jnp
from jax import lax
from jax.experimental import pallas as pl
from jax.experimental.pallas import tpu as pltpu

F32 = jnp.float32
BF16 = jnp.bfloat16
I32 = jnp.int32

D_MODEL = 1024
CHUNK = 64
HEAD_DIM = 64
N_HEADS = 8
MAX_SELECTED_KEYS = 256
ROT_DIM = HEAD_DIM // 4
ROPE_THETA = 500000.0
N_EXPERTS = 32
TOP_K = 4
D_EXPERT = D_MODEL
SWIGLU_LIMIT = 7.0
SWIGLU_ALPHA = 1.702
EXPERT_BLOCK = 256
RMS_EPS = 1e-6

LANES = 128
VMEM_LIMIT = 56 * 1024 * 1024

ROW_TILE = 256
DSA_QB = 128
DSA_KC = 512
SB_T = 256
SB_HEADS = 8
COMBINE_T = 128
DISPATCH_T = 512

NEG_BIG = -1e30
TINY = 1e-30
SB_LOG_ZERO = -120.0


def _nt_dot(a, b):
    return lax.dot_general(a, b, (((1,), (1,)), ((), ())), preferred_element_type=F32)


def _split_bf16(x):
    hi = x.astype(BF16)
    lo = (x - hi.astype(F32)).astype(BF16)
    return hi, lo


_OFF_QA, _OFF_QI, _OFF_QB, _OFF_KB, _OFF_VB = 0, 512, 1024, 1536, 2048
_OFF_KK, _OFF_VW, _OFF_GA, _OFF_GB, _W_COLS = 2560, 2688, 2816, 3840, 4864


def _rope128(y, c, s1, s2):
    return y * c + pltpu.roll(y, 8, 1) * s1 + pltpu.roll(y, LANES - 8, 1) * s2


def _head_rms_inv(x, norm_lo, norm_hi):
    lane = lax.broadcasted_iota(I32, x.shape, 1)
    lo = lane < HEAD_DIM
    sq = x * x
    one = jnp.ones((x.shape[0], 1), F32)
    inv_lo = one
    inv_hi = one
    if norm_lo:
        s = jnp.sum(jnp.where(lo, sq, 0.0), axis=-1, keepdims=True)
        inv_lo = lax.rsqrt(s * (1.0 / HEAD_DIM) + RMS_EPS)
    if norm_hi:
        s = jnp.sum(jnp.where(lo, 0.0, sq), axis=-1, keepdims=True)
        inv_hi = lax.rsqrt(s * (1.0 / HEAD_DIM) + RMS_EPS)
    return jnp.where(lo, inv_lo, inv_hi)


def _inproj_kernel(x_ref, gmix_ref, w_ref, wvt_ref, qg_ref, kg_ref, c_ref, s1_ref, s2_ref,
                   qa_ref, qi_ref, qb_ref, kb_ref, vb_ref, ka_ref, ki_ref, va_ref,
                   wi_ref, ga_ref, gb_ref):
    x = x_ref[...]
    ms = jnp.mean(x * x, axis=-1, keepdims=True)
    xn = (x * lax.rsqrt(ms + RMS_EPS) * gmix_ref[...]).astype(BF16)

    def proj(off, width):
        return jnp.dot(xn, w_ref[:, off:off + width], preferred_element_type=F32)

    c, s1, s2 = c_ref[...], s1_ref[...], s2_ref[...]
    scale = HEAD_DIM ** -0.5

    def put_heads(dst_ref, pair, val):
        dst_ref[2 * pair] = val[:, :HEAD_DIM].astype(BF16)
        dst_ref[2 * pair + 1] = val[:, HEAD_DIM:].astype(BF16)

    acc = proj(_OFF_QA, 512)
    for p in range(4):
        blk = acc[:, p * LANES:(p + 1) * LANES]
        y = blk * _head_rms_inv(blk, True, True) * qg_ref[...]
        put_heads(qa_ref, p, _rope128(y, c, s1, s2) * scale)

    acc = proj(_OFF_QI, 512)
    for p in range(4):
        blk = acc[:, p * LANES:(p + 1) * LANES]
        put_heads(qi_ref, p, _rope128(blk, c, s1, s2) * scale)

    acc = proj(_OFF_QB, 512)
    for p in range(4):
        put_heads(qb_ref, p, acc[:, p * LANES:(p + 1) * LANES] * scale)
    acc = proj(_OFF_KB, 512)
    for p in range(4):
        put_heads(kb_ref, p, acc[:, p * LANES:(p + 1) * LANES])
    acc = proj(_OFF_VB, 512)
    low_half = lax.broadcasted_iota(I32, (x.shape[0], LANES), 1) < HEAD_DIM
    for p in range(4):
        blk = acc[:, p * LANES:(p + 1) * LANES]
        vb_ref[2 * p] = jnp.where(low_half, blk, 0.0).astype(BF16)
        vb_ref[2 * p + 1] = jnp.where(low_half, 0.0, blk).astype(BF16)

    kk = proj(_OFF_KK, LANES)
    y = kk * _head_rms_inv(kk, True, False) * kg_ref[...]
    y = _rope128(y, c, s1, s2)
    ka_ref[...] = y[:, :HEAD_DIM].astype(BF16)
    ki_ref[...] = y[:, HEAD_DIM:].astype(BF16)

    vw_t = _nt_dot(wvt_ref[...], xn)
    va_ref[0] = vw_t[:HEAD_DIM].astype(BF16)
    wi_ref[...] = vw_t[HEAD_DIM:HEAD_DIM + N_HEADS]

    ga_ref[...] = proj(_OFF_GA, D_MODEL)
    gb_ref[...] = proj(_OFF_GB, D_MODEL)


def _rope_tables(seq):
    pos = jnp.arange(seq, dtype=F32)
    inv_freq = ROPE_THETA ** (-jnp.arange(0, ROT_DIM, 2, dtype=F32) / ROT_DIM)
    ang = pos[:, None] * inv_freq[None, :]
    cos, sin = jnp.cos(ang), jnp.sin(ang)
    half = ROT_DIM // 2
    zeros = jnp.zeros((seq, HEAD_DIM - ROT_DIM), F32)
    zh = jnp.zeros((seq, half), F32)
    c64 = jnp.concatenate([cos, cos, zeros + 1.0], axis=1)
    s1_64 = jnp.concatenate([zh, sin, zeros], axis=1)
    s2_64 = jnp.concatenate([-sin, zh, zeros], axis=1)
    tile2 = lambda t: jnp.concatenate([t, t], axis=1)
    return tile2(c64), tile2(s1_64), tile2(s2_64)


def _inproj(x2, norm_g, w_in, q_norm_g, k_norm_g, seq):
    n = x2.shape[0]
    w = w_in
    sl = lambda a, b: w[:, a:b]
    o = [0, 512, 576, 640, 1152, 1216, 1224, 1736, 2248, 2760, 3784, 4808]
    pad = jnp.zeros((D_MODEL, LANES - HEAD_DIM - N_HEADS), w.dtype)
    w_all = jnp.concatenate([
        sl(o[0], o[1]), sl(o[3], o[4]), sl(o[6], o[7]), sl(o[7], o[8]), sl(o[8], o[9]),
        sl(o[1], o[2]), sl(o[4], o[5]),
        sl(o[2], o[3]), sl(o[5], o[6]), pad,
        sl(o[9], o[10]), sl(o[10], o[11])], axis=1).astype(BF16)
    assert w_all.shape[1] == _W_COLS
    qg = jnp.concatenate([q_norm_g, q_norm_g])[None, :]
    kg = jnp.concatenate([k_norm_g, jnp.ones_like(k_norm_g)])[None, :]
    c, s1, s2 = _rope_tables(seq)
    tm = ROW_TILE
    pos_blocks = seq // tm
    full = lambda shape: pl.BlockSpec(shape, lambda i: (0,) * len(shape))
    head_out = jax.ShapeDtypeStruct((N_HEADS, n, HEAD_DIM), BF16)
    head_spec = pl.BlockSpec((N_HEADS, tm, HEAD_DIM), lambda i: (0, i, 0))
    row64 = jax.ShapeDtypeStruct((n, HEAD_DIM), BF16)
    row64_spec = pl.BlockSpec((tm, HEAD_DIM), lambda i: (i, 0))
    tab_spec = pl.BlockSpec((tm, LANES), lambda i: (i % pos_blocks, 0))
    return pl.pallas_call(
        _inproj_kernel,
        grid=(n // tm,),
        in_specs=[pl.BlockSpec((tm, D_MODEL), lambda i: (i, 0)),
                  full((1, D_MODEL)), full((D_MODEL, _W_COLS)), full((LANES, D_MODEL)),
                  full((1, LANES)), full((1, LANES)), tab_spec, tab_spec, tab_spec],
        out_specs=[head_spec] * 4 + [pl.BlockSpec((N_HEADS, tm, LANES), lambda i: (0, i, 0))]
                  + [row64_spec] * 2
                  + [pl.BlockSpec((1, HEAD_DIM, tm), lambda i: (i, 0, 0)),
                     pl.BlockSpec((N_HEADS, tm), lambda i: (0, i)),
                     pl.BlockSpec((tm, D_MODEL), lambda i: (i, 0)),
                     pl.BlockSpec((tm, D_MODEL), lambda i: (i, 0))],
        out_shape=[head_out] * 4 + [jax.ShapeDtypeStruct((N_HEADS, n, LANES), BF16)]
                  + [row64] * 2
                  + [jax.ShapeDtypeStruct((n // tm, HEAD_DIM, tm), BF16),
                     jax.ShapeDtypeStruct((N_HEADS, n), F32),
                     jax.ShapeDtypeStruct((n, D_MODEL), F32),
                     jax.ShapeDtypeStruct((n, D_MODEL), F32)],
        compiler_params=pltpu.CompilerParams(vmem_limit_bytes=VMEM_LIMIT),
        name="inproj",
    )(x2, norm_g[None, :], w_all, w_all[:, _OFF_VW:_OFF_VW + LANES].T, qg, kg, c, s1, s2)


def _key_to_float(u):
    k = u ^ jnp.int32(-2 ** 31)
    b = k ^ ((k >> 31) & jnp.int32(0x7FFFFFFF))
    return lax.bitcast_convert_type(b, F32)


def _float_to_key(x):
    b = lax.bitcast_convert_type(x, I32)
    return b ^ ((b >> 31) & jnp.int32(0x7FFFFFFF)) ^ jnp.int32(-2 ** 31)


def _bit_transpose32(words):
    a = list(words)
    j, m = 16, 0x0000FFFF
    while j:
        k = 0
        while k < 32:
            t = (a[k] ^ lax.shift_right_logical(a[k + j], jnp.int32(j))) & jnp.int32(_as_i32(m))
            a[k] = a[k] ^ t
            a[k + j] = a[k + j] ^ (t << j)
            k = (k + j + 1) & ~j
        j >>= 1
        m = (m ^ (m << j)) & 0xFFFFFFFF
    return a[::-1]


def _as_i32(v):
    return v - (1 << 32) if v >= (1 << 31) else v


def _slab_reduce(x, op):
    parts = [x[j * 8:(j + 1) * 8] for j in range(x.shape[0] // 8)]
    while len(parts) > 1:
        nxt = [op(parts[j], parts[j + 1]) for j in range(0, len(parts) - 1, 2)]
        if len(parts) % 2:
            nxt.append(parts[-1])
        parts = nxt
    return parts[0]


def _dsa_kernel(n_sel, qa_ref, qi_ref, wi_ref, ka_ref, ki_ref, va_ref, o_ref,
                score_ref, planes_ref, alive_ref, m_ref, l_ref, acc_ref):
    i = pl.program_id(1)
    qb, kc = DSA_QB, DSA_KC
    pairs = range(N_HEADS // 2)
    va_pieces = kc // ROW_TILE
    n_chunks = (i * qb + qb + kc - 1) // kc
    t_pos = i * qb + lax.broadcasted_iota(I32, (1, qb), 1)
    limit = (t_pos // CHUNK + 1) * CHUNK
    kio = lax.broadcasted_iota(I32, (kc, qb), 0)

    def pair_rows(ref, p):
        return ref[2 * p:2 * p + 2].reshape(2 * qb, HEAD_DIM)

    def twice(x):
        return jnp.concatenate([x, x], axis=1)

    w = wi_ref[...] * (N_HEADS ** -0.5)
    w2 = [jnp.concatenate([w[2 * p:2 * p + 1], w[2 * p + 1:2 * p + 2]], axis=1) for p in pairs]
    q_idx = [pair_rows(qi_ref, p) for p in pairs]
    groups_per_chunk = kc // (8 * 32)
    n_groups = planes_ref.shape[1]

    def idx_body(c, carry):
        start = pl.multiple_of(c * kc, kc)
        keys = ki_ref[pl.ds(start, kc), :]
        tot = None
        for p in pairs:
            t = jnp.maximum(_nt_dot(keys, q_idx[p]), 0.0) * w2[p]
            tot = t if tot is None else tot + t
        sc = jnp.where(start + kio < limit, tot[:, :qb] + tot[:, qb:], -jnp.inf)
        score_ref[c] = sc
        okey = _float_to_key(sc)
        for g in range(groups_per_chunk):
            slabs = [okey[(g * 32 + j) * 8:(g * 32 + j + 1) * 8] for j in range(32)]
            planes = _bit_transpose32(slabs)
            for bit in range(32):
                planes_ref[bit, groups_per_chunk * c + g] = planes[bit]
        return carry

    lax.fori_loop(0, n_chunks, idx_body, 0)

    def count(pred_fn):
        def body(c, acc):
            return acc + _slab_reduce(pred_fn(c).astype(F32), jnp.add)
        acc = lax.fori_loop(0, n_chunks, body, jnp.zeros((8, qb), F32))
        return jnp.sum(acc, axis=0, keepdims=True)

    def kth_largest(val_fn):
        def body(it, carry):
            u, n_ge = carry
            trial = u | (jnp.int32(1) << (31 - it))
            cand = _key_to_float(trial)
            cnt = count(lambda c: val_fn(c) >= cand)
            take = cnt >= n_sel
            return jnp.where(take, trial, u), jnp.where(take, cnt, n_ge)
        u, n_ge = lax.fori_loop(0, 32, body,
                                (jnp.zeros((1, qb), I32), jnp.full((1, qb), n_sel, F32)))
        return _key_to_float(u), n_ge

    def kth_largest_key():
        used = groups_per_chunk * n_chunks
        for g in range(n_groups):
            @pl.when(g >= used)
            def _():
                for bit in range(32):
                    planes_ref[bit, g] = jnp.zeros((8, qb), I32)
            alive_ref[g] = jnp.where(g < used, jnp.int32(-1), jnp.int32(0)) + jnp.zeros((8, qb), I32)

        def body(it, carry):
            u, want = carry
            bit = 31 - it
            cnt8 = jnp.zeros((8, qb), I32)
            for g in range(n_groups):
                cnt8 = cnt8 + lax.population_count(alive_ref[g] & planes_ref[bit, g])
            cnt = jnp.sum(cnt8, axis=0, keepdims=True)
            take = cnt >= want
            flip = jnp.where(take, jnp.int32(0), jnp.int32(-1))
            for g in range(n_groups):
                alive_ref[g] = alive_ref[g] & (planes_ref[bit, g] ^ flip)
            return (jnp.where(take, u | (jnp.int32(1) << bit), u),
                    jnp.where(take, want, want - cnt))
        u, _ = lax.fori_loop(0, 32, body,
                             (jnp.zeros((1, qb), I32), jnp.full((1, qb), n_sel, I32)))
        return _key_to_float(u)

    select_all = limit <= n_sel
    thr = kth_largest_key()
    n_ge = count(lambda c: score_ref[c] >= thr)
    n_gt = count(lambda c: score_ref[c] > thr)
    settled = select_all | ((n_gt < n_sel) & (n_ge >= n_sel))
    thr, n_ge = lax.cond(jnp.min(jnp.where(settled, 1, 0)) > 0,
                         lambda: (thr, n_ge), lambda: kth_largest(lambda c: score_ref[c]))
    thr = jnp.where(select_all, -jnp.inf, thr)

    def to_excess(c, carry):
        s = score_ref[c]
        above = jnp.where(s > thr, jnp.maximum(s - thr, TINY), 0.0)
        score_ref[c] = jnp.where(s >= thr, above, -jnp.inf)
        return carry
    lax.fori_loop(0, n_chunks, to_excess, 0)

    def refine():
        n_above = count(lambda c: score_ref[c] > 0.0)
        crowded = jnp.max(jnp.where((n_above >= n_sel) & jnp.logical_not(select_all), 1, 0)) > 0

        def split():
            t2, _ = kth_largest(lambda c: score_ref[c])
            return t2, n_sel - count(lambda c: score_ref[c] > t2)
        thr2, need = lax.cond(crowded, split, lambda: (jnp.zeros((1, qb), F32), n_sel - n_above))

        def tie_body(it, j):
            trial = j | (jnp.int32(1) << (12 - it))
            g = count(lambda c: (score_ref[c] == thr2) & (c * kc + kio < trial))
            return jnp.where(g <= need, trial, j)
        return thr2, lax.fori_loop(0, 13, tie_body, jnp.zeros((1, qb), I32))

    tied = jnp.max(jnp.where((n_ge > n_sel) & jnp.logical_not(select_all), 1, 0)) > 0
    thr2, j_lim = lax.cond(tied, refine, lambda: (jnp.zeros((1, qb), F32),
                                                  jnp.full((1, qb), 2 ** 30, I32)))
    thr2 = jnp.where(select_all, 0.0, thr2)
    j_lim = jnp.where(select_all, jnp.int32(2 ** 30), j_lim)

    q_att = [pair_rows(qa_ref, p) for p in pairs]
    m_ref[...] = jnp.full_like(m_ref, NEG_BIG)
    l_ref[...] = jnp.zeros_like(l_ref)
    acc_ref[...] = jnp.zeros_like(acc_ref)

    def att_body(c, carry):
        start = pl.multiple_of(c * kc, kc)
        kpos = start + kio
        ex = score_ref[c]
        sel = (kpos < limit) & ((ex > thr2) | ((ex == thr2) & (kpos < j_lim)))
        bias = twice(jnp.where(sel, 0.0, NEG_BIG))
        keys = ka_ref[pl.ds(start, kc), :]
        s = [_nt_dot(keys, q_att[p]) + bias for p in pairs]
        m_old = [m_ref[p] for p in pairs]
        m_new = [jnp.maximum(m_old[p], jnp.max(_slab_reduce(s[p], jnp.maximum), axis=0,
                                               keepdims=True)) for p in pairs]
        alpha = [jnp.exp(m_old[p] - m_new[p]) for p in pairs]
        pe = [jnp.exp(s[p] - m_new[p]) for p in pairs]
        pv = []
        for p in pairs:
            pb = pe[p].astype(BF16)
            tot = None
            for v in range(va_pieces):
                part = jnp.dot(va_ref[va_pieces * c + v], pb[v * ROW_TILE:(v + 1) * ROW_TILE],
                               preferred_element_type=F32)
                tot = part if tot is None else tot + part
            pv.append(tot)
        for p in pairs:
            l_ref[p] = alpha[p] * l_ref[p] + jnp.sum(_slab_reduce(pe[p], jnp.add), axis=0,
                                                     keepdims=True)
            acc_ref[p] = acc_ref[p] * alpha[p] + pv[p]
            m_ref[p] = m_new[p]
        return carry

    lax.fori_loop(0, n_chunks, att_body, 0)
    for p in pairs:
        out = acc_ref[p] / l_ref[p]
        both = jnp.concatenate([out[:, :qb], out[:, qb:]], axis=0)
        o_ref[:, 2 * p * HEAD_DIM:2 * (p + 1) * HEAD_DIM] = both.T.astype(BF16)


def _dsa(qa, qi, wi, ka, ki, va, batch, seq):
    n = batch * seq
    nqb = seq // DSA_QB
    n_sel = min(MAX_SELECTED_KEYS, seq // 4)
    head_spec = pl.BlockSpec((N_HEADS, DSA_QB, HEAD_DIM), lambda b, i: (0, b * nqb + i, 0))
    kv_spec = pl.BlockSpec((seq, HEAD_DIM), lambda b, i: (b, 0))
    pair_stat = pltpu.VMEM((N_HEADS // 2, 1, 2 * DSA_QB), F32)
    return pl.pallas_call(
        functools.partial(_dsa_kernel, n_sel),
        grid=(batch, nqb),
        in_specs=[head_spec, head_spec,
                  pl.BlockSpec((N_HEADS, DSA_QB), lambda b, i: (0, b * nqb + i)),
                  kv_spec, kv_spec,
                  pl.BlockSpec((seq // ROW_TILE, HEAD_DIM, ROW_TILE), lambda b, i: (b, 0, 0))],
        out_specs=pl.BlockSpec((DSA_QB, N_HEADS * HEAD_DIM), lambda b, i: (b * nqb + i, 0)),
        out_shape=jax.ShapeDtypeStruct((n, N_HEADS * HEAD_DIM), BF16),
        scratch_shapes=[pltpu.VMEM((seq // DSA_KC, DSA_KC, DSA_QB), F32),
                        pltpu.VMEM((32, seq // 256, 8, DSA_QB), I32),
                        pltpu.VMEM((seq // 256, 8, DSA_QB), I32), pair_stat, pair_stat,
                        pltpu.VMEM((N_HEADS // 2, HEAD_DIM, 2 * DSA_QB), F32)],
        compiler_params=pltpu.CompilerParams(vmem_limit_bytes=VMEM_LIMIT),
        name="dsa",
    )(qa, qi, wi, ka, ki, va)


def _sb_kernel(q_ref, k_ref, v_ref, o_ref):
    i = pl.program_id(2)
    t = SB_T
    row = lax.broadcasted_iota(I32, (t, t), 0)
    col = lax.broadcasted_iota(I32, (t, t), 1)
    tri = jnp.where(row > col, 1.0, 0.0).astype(BF16)
    causal = col < row

    def block(j, state, diag):
        heads = range(SB_HEADS)
        start = pl.multiple_of(j * t, t)
        zs = [_nt_dot(q_ref[hd], k_ref[hd, pl.ds(start, t), :]) for hd in heads]
        lks = []
        for z in zs:
            lk = -(jnp.maximum(z, 0.0) + jnp.log(1.0 + jnp.exp(-jnp.abs(z))))
            lks.append(jnp.where(causal, lk, 0.0) if diag else lk)
        parts = [_split_bf16(lk) for lk in lks]
        laters = [jnp.dot(hi, tri, preferred_element_type=F32)
                  + jnp.dot(lo, tri, preferred_element_type=F32)
                  for hi, lo in parts]
        out = []
        for hd in heads:
            carry, acc = state[hd]
            a = jnp.exp(zs[hd] + lks[hd] + laters[hd] + carry)
            if diag:
                a = jnp.where(causal, a, 0.0)
            acc = acc + jnp.dot(a.astype(BF16), v_ref[hd, pl.ds(start, t), :],
                                preferred_element_type=F32)
            out.append((carry + jnp.sum(lks[hd], axis=-1, keepdims=True), acc))
        return tuple(out)

    state = tuple((jnp.zeros((t, 1), F32), jnp.zeros((t, LANES), F32))
                  for _ in range(SB_HEADS))
    state = block(i, state, True)

    def live(loop):
        jj, st = loop
        top = st[0][0]
        for hd in range(1, SB_HEADS):
            top = jnp.maximum(top, st[hd][0])
        return (jj < i) & (jnp.max(top) > SB_LOG_ZERO)

    _, state = lax.while_loop(live, lambda loop: (loop[0] + 1, block(i - 1 - loop[0], loop[1], False)),
                              (jnp.int32(0), state))
    for p in range(SB_HEADS // 2):
        pair = state[2 * p][1] + state[2 * p + 1][1]
        o_ref[:, p * LANES:(p + 1) * LANES] = pair.astype(BF16)


def _sb(qb, kb, vb, batch, seq):
    n = batch * seq
    nq = seq // SB_T
    assert SB_HEADS == N_HEADS
    q_spec = pl.BlockSpec((SB_HEADS, SB_T, HEAD_DIM), lambda b, h, i: (h, b * nq + i, 0))
    k_spec = pl.BlockSpec((SB_HEADS, seq, HEAD_DIM), lambda b, h, i: (h, b, 0))
    v_spec = pl.BlockSpec((SB_HEADS, seq, LANES), lambda b, h, i: (h, b, 0))
    return pl.pallas_call(
        _sb_kernel,
        grid=(batch, N_HEADS // SB_HEADS, nq),
        in_specs=[q_spec, k_spec, v_spec],
        out_specs=pl.BlockSpec((SB_T, N_HEADS * HEAD_DIM), lambda b, h, i: (b * nq + i, 0)),
        out_shape=jax.ShapeDtypeStruct((n, N_HEADS * HEAD_DIM), BF16),
        compiler_params=pltpu.CompilerParams(vmem_limit_bytes=VMEM_LIMIT),
        name="stickbreak",
    )(qb, kb, vb)


def _outproj_kernel(x_ref, oa_ref, ob_ref, ga_ref, gb_ref, wpa_ref, wpb_ref, wo_ref,
                    gffn_ref, wr_hi_ref, wr_lo_ref, br_ref,
                    h_ref, hn_ref, idx_ref, gate_ref, rank_ref, cnt_ref, carry_ref):
    step = pl.program_id(0)
    tm = x_ref.shape[0]

    @pl.when(step == 0)
    def _():
        carry_ref[...] = jnp.zeros_like(carry_ref)

    pa = jnp.dot(oa_ref[...], wpa_ref[...], preferred_element_type=F32)
    pb = jnp.dot(ob_ref[...], wpb_ref[...], preferred_element_type=F32)
    merged = jax.nn.sigmoid(ga_ref[...]) * pa + jax.nn.sigmoid(gb_ref[...]) * pb
    h = x_ref[...] + jnp.dot(merged.astype(BF16), wo_ref[...], preferred_element_type=F32)
    h_ref[...] = h
    ms = jnp.mean(h * h, axis=-1, keepdims=True)
    hn = h * lax.rsqrt(ms + RMS_EPS) * gffn_ref[...]
    hn_ref[...] = hn

    hn_hi, hn_lo = _split_bf16(hn)
    logits = (jnp.dot(hn_hi, wr_hi_ref[...], preferred_element_type=F32)
              + jnp.dot(hn_lo, wr_hi_ref[...], preferred_element_type=F32)
              + jnp.dot(hn_hi, wr_lo_ref[...], preferred_element_type=F32)) + br_ref[...]

    lane = lax.broadcasted_iota(I32, (tm, N_EXPERTS), 1)
    lane_k = lax.broadcasted_iota(I32, (tm, TOP_K), 1)
    vals = logits
    picks, top_vals, top_idx = [], [], []
    for _ in range(TOP_K):
        mx = jnp.max(vals, axis=-1, keepdims=True)
        first = jnp.min(jnp.where(vals == mx, lane, N_EXPERTS), axis=-1, keepdims=True)
        pick = lane == first
        picks.append(pick)
        top_vals.append(mx)
        top_idx.append(first)
        vals = jnp.where(pick, -jnp.inf, vals)
    exps = [jnp.exp(v - top_vals[0]) for v in top_vals]
    denom = exps[0] + exps[1] + exps[2] + exps[3]

    any_pick = picks[0] | picks[1] | picks[2] | picks[3]
    onehot = jnp.where(any_pick, 1.0, 0.0)
    r_i = lax.broadcasted_iota(I32, (tm, tm), 0)
    c_i = lax.broadcasted_iota(I32, (tm, tm), 1)
    lower = jnp.where(c_i < r_i, 1.0, 0.0).astype(BF16)
    pos = jnp.dot(lower, onehot.astype(BF16), preferred_element_type=F32) + carry_ref[...]
    carry_ref[...] = carry_ref[...] + jnp.sum(onehot, axis=0, keepdims=True)
    cnt_ref[...] = carry_ref[...].astype(I32)

    idx4 = jnp.zeros((tm, TOP_K), I32)
    gate4 = jnp.zeros((tm, TOP_K), F32)
    rank4 = jnp.zeros((tm, TOP_K), I32)
    for k in range(TOP_K):
        rk = jnp.sum(jnp.where(picks[k], pos, 0.0), axis=-1, keepdims=True).astype(I32)
        idx4 = jnp.where(lane_k == k, top_idx[k], idx4)
        gate4 = jnp.where(lane_k == k, exps[k] / denom, gate4)
        rank4 = jnp.where(lane_k == k, rk, rank4)
    idx_ref[...] = idx4
    gate_ref[...] = gate4
    rank_ref[...] = rank4


def _outproj(x2, oa, ob, ga, gb, w_proj_a, w_proj_b, w_out, norm_ffn_g, w_router, b_router):
    n = x2.shape[0]
    tm = ROW_TILE
    width = N_HEADS * HEAD_DIM
    wpa = w_proj_a.astype(BF16)
    wpb = w_proj_b.astype(BF16)
    wr_hi = w_router.astype(BF16)
    wr_lo = (w_router - wr_hi.astype(F32)).astype(BF16)
    full = lambda shape: pl.BlockSpec(shape, lambda i: (0,) * len(shape))
    row = lambda width: pl.BlockSpec((tm, width), lambda i: (i, 0))
    return pl.pallas_call(
        _outproj_kernel,
        grid=(n // tm,),
        in_specs=[row(D_MODEL), row(width), row(width), row(D_MODEL), row(D_MODEL),
                  full((width, D_MODEL)), full((width, D_MODEL)),
                  full((D_MODEL, D_MODEL)), full((1, D_MODEL)),
                  full((D_MODEL, N_EXPERTS)), full((D_MODEL, N_EXPERTS)), full((1, N_EXPERTS))],
        out_specs=[row(D_MODEL), row(D_MODEL), row(TOP_K), row(TOP_K), row(TOP_K),
                   full((1, N_EXPERTS))],
        out_shape=[jax.ShapeDtypeStruct((n, D_MODEL), F32), jax.ShapeDtypeStruct((n, D_MODEL), F32),
                   jax.ShapeDtypeStruct((n, TOP_K), I32), jax.ShapeDtypeStruct((n, TOP_K), F32),
                   jax.ShapeDtypeStruct((n, TOP_K), I32), jax.ShapeDtypeStruct((1, N_EXPERTS), I32)],
        scratch_shapes=[pltpu.VMEM((1, N_EXPERTS), F32)],
        compiler_params=pltpu.CompilerParams(vmem_limit_bytes=VMEM_LIMIT,
                                             dimension_semantics=("arbitrary",)),
        name="outproj_router",
    )(x2, oa, ob, ga, gb, wpa, wpb, w_out.astype(BF16), norm_ffn_g[None, :],
      wr_hi, wr_lo, b_router[None, :])


def _dispatch_kernel(dest_ref, fill_lo_ref, fill_hi_ref, hn_ref, xs_ref, zero_ref, sem, zsem):
    s = pl.program_id(0)
    dt = hn_ref.shape[0]

    def tok(tt, carry):
        t = s * dt + tt
        for k in range(TOP_K):
            d = dest_ref[t * TOP_K + k]
            pltpu.make_async_copy(hn_ref.at[pl.ds(tt, 1)], xs_ref.at[pl.ds(d, 1)],
                                  sem).start(priority=k % 2)
        return carry
    lax.fori_loop(0, dt, tok, 0, unroll=4)

    @pl.when(s == 0)
    def _():
        zero_ref[...] = jnp.zeros_like(zero_ref)

        def expert(e, total):
            lo, hi = fill_lo_ref[e], fill_hi_ref[e]

            def fill(r, carry):
                pltpu.make_async_copy(zero_ref, xs_ref.at[pl.ds(r, 1)], zsem).start()
                return carry
            lax.fori_loop(lo, hi, fill, 0)
            return total + (hi - lo)
        total = lax.fori_loop(0, fill_lo_ref.shape[0], expert, jnp.int32(0))

        def drain(r, carry):
            pltpu.make_async_copy(zero_ref, xs_ref.at[pl.ds(0, 1)], zsem).wait()
            return carry
        lax.fori_loop(0, total, drain, 0)

    for _ in range(TOP_K):
        pltpu.make_async_copy(hn_ref, xs_ref.at[pl.ds(0, dt)], sem).wait()


def _dispatch(dest_flat, fill_lo, fill_hi, hn, rows):
    n = hn.shape[0]
    dt = DISPATCH_T
    return pl.pallas_call(
        _dispatch_kernel,
        grid_spec=pltpu.PrefetchScalarGridSpec(
            num_scalar_prefetch=3,
            grid=(n // dt,),
            in_specs=[pl.BlockSpec((dt, D_MODEL), lambda i, d, lo, hi: (i, 0))],
            out_specs=pl.BlockSpec(memory_space=pl.ANY),
            scratch_shapes=[pltpu.VMEM((1, D_MODEL), F32),
                            pltpu.SemaphoreType.DMA(()), pltpu.SemaphoreType.DMA(())]),
        out_shape=jax.ShapeDtypeStruct((rows, D_MODEL), F32),
        compiler_params=pltpu.CompilerParams(vmem_limit_bytes=VMEM_LIMIT,
                                             dimension_semantics=("arbitrary",),
                                             has_side_effects=True),
        name="dispatch",
    )(dest_flat, fill_lo, fill_hi, hn)


def _expert_kernel(be_ref, nu_ref, x_ref, wgu_ref, bgu_ref, wd_ref, bd_ref, y_ref,
                   wgu_bf, wd_bf):
    b = pl.program_id(0)
    n_used = nu_ref[0]
    prev = be_ref[jnp.maximum(b - 1, 0)]

    @pl.when((b < n_used) & ((b == 0) | (be_ref[b] != prev)))
    def _():
        wgu_bf[...] = wgu_ref[0].astype(BF16)
        wd_bf[...] = wd_ref[0].astype(BF16)

    @pl.when(b < n_used)
    def _():
        xb = x_ref[...].astype(BF16)
        hgu = jnp.dot(xb, wgu_bf[...], preferred_element_type=F32) + bgu_ref[0]
        gate = jnp.minimum(hgu[:, :D_EXPERT], SWIGLU_LIMIT)
        up = jnp.clip(hgu[:, D_EXPERT:], -SWIGLU_LIMIT, SWIGLU_LIMIT)
        act = (up + 1.0) * gate * jax.nn.sigmoid(SWIGLU_ALPHA * gate)
        y_ref[...] = (jnp.dot(act.astype(BF16), wd_bf[...], preferred_element_type=F32)
                      + bd_ref[0])

    @pl.when(b >= n_used)
    def _():
        y_ref[...] = jnp.zeros_like(y_ref)


def _experts(block_expert, n_used, xs, w_gu, b_gu, w_down, b_down):
    rows = xs.shape[0]
    n_blocks = rows // EXPERT_BLOCK

    def blk(b, be, nu):
        return jnp.minimum(b, nu[0] - 1)

    w_spec = lambda d1, d2: pl.BlockSpec((1, d1, d2), lambda b, be, nu: (be[blk(b, be, nu)], 0, 0))
    return pl.pallas_call(
        _expert_kernel,
        grid_spec=pltpu.PrefetchScalarGridSpec(
            num_scalar_prefetch=2,
            grid=(n_blocks,),
            in_specs=[pl.BlockSpec((EXPERT_BLOCK, D_MODEL), lambda b, be, nu: (blk(b, be, nu), 0)),
                      w_spec(D_MODEL, 2 * D_EXPERT), w_spec(1, 2 * D_EXPERT),
                      w_spec(D_EXPERT, D_MODEL), w_spec(1, D_MODEL)],
            out_specs=pl.BlockSpec((EXPERT_BLOCK, D_MODEL), lambda b, be, nu: (b, 0)),
            scratch_shapes=[pltpu.VMEM((D_MODEL, 2 * D_EXPERT), BF16),
                            pltpu.VMEM((D_EXPERT, D_MODEL), BF16)]),
        out_shape=jax.ShapeDtypeStruct((rows, D_MODEL), F32),
        compiler_params=pltpu.CompilerParams(vmem_limit_bytes=VMEM_LIMIT,
                                             dimension_semantics=("arbitrary",)),
        name="experts",
    )(block_expert, n_used, xs, w_gu, b_gu[:, None, :], w_down, b_down[:, None, :])


def _combine_kernel(dest_ref, h_ref, gate_ref, y_ref, o_ref, buf_ref, sems):
    s = pl.program_id(0)
    n_steps = pl.num_programs(0)
    ct = COMBINE_T

    def issue(step, slot):
        def tok(tt, carry):
            t = step * ct + tt
            for k in range(TOP_K):
                d = dest_ref[t * TOP_K + k]
                pltpu.make_async_copy(y_ref.at[pl.ds(d, 1)], buf_ref.at[slot, k, pl.ds(tt, 1)],
                                      sems.at[slot]).start(priority=k % 2)
            return carry
        lax.fori_loop(0, ct, tok, 0, unroll=4)

    @pl.when(s == 0)
    def _():
        issue(0, 0)

    for parity in range(2):
        @pl.when((s + 1 < n_steps) & ((s + 1) % 2 == parity))
        def _():
            issue(s + 1, parity)

    slot = s % 2
    for k in range(TOP_K):
        pltpu.make_async_copy(y_ref.at[pl.ds(0, ct)], buf_ref.at[slot, k], sems.at[slot]).wait()
    g = gate_ref[...]
    out = h_ref[...]
    for k in range(TOP_K):
        out = out + g[:, k:k + 1] * buf_ref[slot, k]
    o_ref[...] = out


def _combine(dest_flat, h, gates, y):
    n = h.shape[0]
    ct = COMBINE_T
    return pl.pallas_call(
        _combine_kernel,
        grid_spec=pltpu.PrefetchScalarGridSpec(
            num_scalar_prefetch=1,
            grid=(n // ct,),
            in_specs=[pl.BlockSpec((ct, D_MODEL), lambda i, d: (i, 0)),
                      pl.BlockSpec((ct, TOP_K), lambda i, d: (i, 0)),
                      pl.BlockSpec(memory_space=pl.ANY)],
            out_specs=pl.BlockSpec((ct, D_MODEL), lambda i, d: (i, 0)),
            scratch_shapes=[pltpu.VMEM((2, TOP_K, ct, D_MODEL), F32),
                            pltpu.SemaphoreType.DMA((2,))]),
        out_shape=jax.ShapeDtypeStruct((n, D_MODEL), F32),
        compiler_params=pltpu.CompilerParams(vmem_limit_bytes=VMEM_LIMIT,
                                             dimension_semantics=("arbitrary",)),
        name="combine",
    )(dest_flat, h, gates, y)


def _moe(h, hn, top_idx, gates, rank, counts, w_gate_up, b_gate_up, w_down, b_down):
    n = h.shape[0]
    m = n * TOP_K
    n_blocks = (m + N_EXPERTS * (EXPERT_BLOCK - 1) + EXPERT_BLOCK - 1) // EXPERT_BLOCK
    rows = n_blocks * EXPERT_BLOCK
    counts = counts[0]
    padded = ((counts + EXPERT_BLOCK - 1) // EXPERT_BLOCK) * EXPERT_BLOCK
    pend = jnp.cumsum(padded).astype(I32)
    pstart = pend - padded
    dest = (pstart[top_idx] + rank).reshape(m).astype(I32)
    block_start = jnp.arange(n_blocks, dtype=I32) * EXPERT_BLOCK
    block_expert = jnp.minimum(jnp.sum(pend[None, :] <= block_start[:, None], axis=1),
                               N_EXPERTS - 1).astype(I32)
    n_used = (pend[-1:] // EXPERT_BLOCK).astype(I32)
    fill_lo = jnp.concatenate([pstart + counts, pend[-1:]]).astype(I32)
    fill_hi = jnp.concatenate([pend, jnp.full((1,), rows, I32)])
    xs = _dispatch(dest, fill_lo, fill_hi, hn, rows)
    y = _experts(block_expert, n_used, xs, w_gate_up, b_gate_up, w_down, b_down)
    return _combine(dest, h, gates, y)


def kernel(x, norm_mix_g, w_in, q_norm_g, k_norm_g, w_proj_a, w_proj_b, w_out, norm_ffn_g,
           w_router, b_router, w_gate_up, b_gate_up, w_down, b_down):
    batch, seq, d = x.shape
    h = x.reshape(batch * seq, d)
    for l in range(norm_mix_g.shape[0]):
        (qa, qi, qb, kb, vb, ka, ki, va, wi, ga, gb) = _inproj(
            h, norm_mix_g[l], w_in[l], q_norm_g[l], k_norm_g[l], seq)
        oa = _dsa(qa, qi, wi, ka, ki, va, batch, seq)
        ob = _sb(qb, kb, vb, batch, seq)
        h_mid, hn, top_idx, gates, rank, counts = _outproj(
            h, oa, ob, ga, gb, w_proj_a[l], w_proj_b[l], w_out[l], norm_ffn_g[l],
            w_router[l], b_router[l])
        h = _moe(h_mid, hn, top_idx, gates, rank, counts,
                 w_gate_up[l], b_gate_up[l], w_down[l], b_down[l])
    return h.reshape(batch, seq, d)
```

```python
import functools

import jax
import jax.numpy as jnp
from jax import lax
from jax.experimental import pallas as pl
from jax.experimental.pallas import tpu as pltpu

F32 = jnp.float32
BF16 = jnp.bfloat16
I32 = jnp.int32

D_MODEL = 1024
CHUNK = 64
HEAD_DIM = 64
N_HEADS = 8
MAX_SELECTED_KEYS = 256
ROT_DIM = HEAD_DIM // 4
ROPE_THETA = 500000.0
N_EXPERTS = 32
TOP_K = 4
D_EXPERT = D_MODEL
SWIGLU_LIMIT = 7.0
SWIGLU_ALPHA = 1.702
EXPERT_BLOCK = 256
RMS_EPS = 1e-6

LANES = 128
VMEM_LIMIT = 56 * 1024 * 1024

ROW_TILE = 256
DSA_QB = 256
DSA_KC = 512
SB_T = 256
SB_HEADS = 8
COMBINE_T = 128
DISPATCH_T = 512

NEG_BIG = -1e30
TINY = 1e-30
SB_LOG_ZERO = -120.0


def _nt_dot(a, b):
    return lax.dot_general(a, b, (((1,), (1,)), ((), ())), preferred_element_type=F32)


def _split_bf16(x):
    hi = x.astype(BF16)
    lo = (x - hi.astype(F32)).astype(BF16)
    return hi, lo


_OFF_QA, _OFF_QI, _OFF_QB, _OFF_KB, _OFF_VB = 0, 512, 1024, 1536, 2048
_OFF_KK, _OFF_VW, _OFF_GA, _OFF_GB, _W_COLS = 2560, 2688, 2816, 3840, 4864


def _rope128(y, c, s1, s2):
    return y * c + pltpu.roll(y, 8, 1) * s1 + pltpu.roll(y, LANES - 8, 1) * s2


def _head_rms_inv(x, norm_lo, norm_hi):
    lane = lax.broadcasted_iota(I32, x.shape, 1)
    lo = lane < HEAD_DIM
    sq = x * x
    one = jnp.ones((x.shape[0], 1), F32)
    inv_lo = one
    inv_hi = one
    if norm_lo:
        s = jnp.sum(jnp.where(lo, sq, 0.0), axis=-1, keepdims=True)
        inv_lo = lax.rsqrt(s * (1.0 / HEAD_DIM) + RMS_EPS)
    if norm_hi:
        s = jnp.sum(jnp.where(lo, 0.0, sq), axis=-1, keepdims=True)
        inv_hi = lax.rsqrt(s * (1.0 / HEAD_DIM) + RMS_EPS)
    return jnp.where(lo, inv_lo, inv_hi)


def _inproj_kernel(x_ref, gmix_ref, w_ref, wvt_ref, qg_ref, kg_ref, c_ref, s1_ref, s2_ref,
                   qa_ref, qi_ref, qb_ref, kb_ref, vb_ref, ka_ref, ki_ref, va_ref,
                   wi_ref, ga_ref, gb_ref):
    x = x_ref[...]
    ms = jnp.mean(x * x, axis=-1, keepdims=True)
    xn = (x * lax.rsqrt(ms + RMS_EPS) * gmix_ref[...]).astype(BF16)

    def proj(off, width):
        return jnp.dot(xn, w_ref[:, off:off + width], preferred_element_type=F32)

    c, s1, s2 = c_ref[...], s1_ref[...], s2_ref[...]
    scale = HEAD_DIM ** -0.5

    def put_heads(dst_ref, pair, val):
        dst_ref[2 * pair] = val[:, :HEAD_DIM].astype(BF16)
        dst_ref[2 * pair + 1] = val[:, HEAD_DIM:].astype(BF16)

    acc = proj(_OFF_QA, 512)
    for p in range(4):
        blk = acc[:, p * LANES:(p + 1) * LANES]
        y = blk * _head_rms_inv(blk, True, True) * qg_ref[...]
        put_heads(qa_ref, p, _rope128(y, c, s1, s2) * scale)

    acc = proj(_OFF_QI, 512)
    for p in range(4):
        blk = acc[:, p * LANES:(p + 1) * LANES]
        put_heads(qi_ref, p, _rope128(blk, c, s1, s2) * scale)

    acc = proj(_OFF_QB, 512)
    for p in range(4):
        put_heads(qb_ref, p, acc[:, p * LANES:(p + 1) * LANES] * scale)
    acc = proj(_OFF_KB, 512)
    for p in range(4):
        put_heads(kb_ref, p, acc[:, p * LANES:(p + 1) * LANES])
    acc = proj(_OFF_VB, 512)
    low_half = lax.broadcasted_iota(I32, (x.shape[0], LANES), 1) < HEAD_DIM
    for p in range(4):
        blk = acc[:, p * LANES:(p + 1) * LANES]
        vb_ref[2 * p] = jnp.where(low_half, blk, 0.0).astype(BF16)
        vb_ref[2 * p + 1] = jnp.where(low_half, 0.0, blk).astype(BF16)

    kk = proj(_OFF_KK, LANES)
    y = kk * _head_rms_inv(kk, True, False) * kg_ref[...]
    y = _rope128(y, c, s1, s2)
    ka_ref[...] = y[:, :HEAD_DIM].astype(BF16)
    ki_ref[...] = y[:, HEAD_DIM:].astype(BF16)

    vw_t = _nt_dot(wvt_ref[...], xn)
    va_ref[0] = vw_t[:HEAD_DIM].astype(BF16)
    wi_ref[...] = vw_t[HEAD_DIM:HEAD_DIM + N_HEADS]

    ga_ref[...] = proj(_OFF_GA, D_MODEL)
    gb_ref[...] = proj(_OFF_GB, D_MODEL)


def _rope_tables(seq):
    pos = jnp.arange(seq, dtype=F32)
    inv_freq = ROPE_THETA ** (-jnp.arange(0, ROT_DIM, 2, dtype=F32) / ROT_DIM)
    ang = pos[:, None] * inv_freq[None, :]
    cos, sin = jnp.cos(ang), jnp.sin(ang)
    half = ROT_DIM // 2
    zeros = jnp.zeros((seq, HEAD_DIM - ROT_DIM), F32)
    zh = jnp.zeros((seq, half), F32)
    c64 = jnp.concatenate([cos, cos, zeros + 1.0], axis=1)
    s1_64 = jnp.concatenate([zh, sin, zeros], axis=1)
    s2_64 = jnp.concatenate([-sin, zh, zeros], axis=1)
    tile2 = lambda t: jnp.concatenate([t, t], axis=1)
    return tile2(c64), tile2(s1_64), tile2(s2_64)


def _inproj(x2, norm_g, w_in, q_norm_g, k_norm_g, seq):
    n = x2.shape[0]
    w = w_in
    sl = lambda a, b: w[:, a:b]
    o = [0, 512, 576, 640, 1152, 1216, 1224, 1736, 2248, 2760, 3784, 4808]
    pad = jnp.zeros((D_MODEL, LANES - HEAD_DIM - N_HEADS), w.dtype)
    w_all = jnp.concatenate([
        sl(o[0], o[1]), sl(o[3], o[4]), sl(o[6], o[7]), sl(o[7], o[8]), sl(o[8], o[9]),
        sl(o[1], o[2]), sl(o[4], o[5]),
        sl(o[2], o[3]), sl(o[5], o[6]), pad,
        sl(o[9], o[10]), sl(o[10], o[11])], axis=1).astype(BF16)
    assert w_all.shape[1] == _W_COLS
    qg = jnp.concatenate([q_norm_g, q_norm_g])[None, :]
    kg = jnp.concatenate([k_norm_g, jnp.ones_like(k_norm_g)])[None, :]
    c, s1, s2 = _rope_tables(seq)
    tm = ROW_TILE
    pos_blocks = seq // tm
    full = lambda shape: pl.BlockSpec(shape, lambda i: (0,) * len(shape))
    head_out = jax.ShapeDtypeStruct((N_HEADS, n, HEAD_DIM), BF16)
    head_spec = pl.BlockSpec((N_HEADS, tm, HEAD_DIM), lambda i: (0, i, 0))
    row64 = jax.ShapeDtypeStruct((n, HEAD_DIM), BF16)
    row64_spec = pl.BlockSpec((tm, HEAD_DIM), lambda i: (i, 0))
    tab_spec = pl.BlockSpec((tm, LANES), lambda i: (i % pos_blocks, 0))
    return pl.pallas_call(
        _inproj_kernel,
        grid=(n // tm,),
        in_specs=[pl.BlockSpec((tm, D_MODEL), lambda i: (i, 0)),
                  full((1, D_MODEL)), full((D_MODEL, _W_COLS)), full((LANES, D_MODEL)),
                  full((1, LANES)), full((1, LANES)), tab_spec, tab_spec, tab_spec],
        out_specs=[head_spec] * 4 + [pl.BlockSpec((N_HEADS, tm, LANES), lambda i: (0, i, 0))]
                  + [row64_spec] * 2
                  + [pl.BlockSpec((1, HEAD_DIM, tm), lambda i: (i, 0, 0)),
                     pl.BlockSpec((N_HEADS, tm), lambda i: (0, i)),
                     pl.BlockSpec((tm, D_MODEL), lambda i: (i, 0)),
                     pl.BlockSpec((tm, D_MODEL), lambda i: (i, 0))],
        out_shape=[head_out] * 4 + [jax.ShapeDtypeStruct((N_HEADS, n, LANES), BF16)]
                  + [row64] * 2
                  + [jax.ShapeDtypeStruct((n // tm, HEAD_DIM, tm), BF16),
                     jax.ShapeDtypeStruct((N_HEADS, n), F32),
                     jax.ShapeDtypeStruct((n, D_MODEL), F32),
                     jax.ShapeDtypeStruct((n, D_MODEL), F32)],
        compiler_params=pltpu.CompilerParams(vmem_limit_bytes=VMEM_LIMIT),
        name="inproj",
    )(x2, norm_g[None, :], w_all, w_all[:, _OFF_VW:_OFF_VW + LANES].T, qg, kg, c, s1, s2)


def _key_to_float(u):
    k = u ^ jnp.int32(-2 ** 31)
    b = k ^ ((k >> 31) & jnp.int32(0x7FFFFFFF))
    return lax.bitcast_convert_type(b, F32)


def _float_to_key(x):
    b = lax.bitcast_convert_type(x, I32)
    return b ^ ((b >> 31) & jnp.int32(0x7FFFFFFF)) ^ jnp.int32(-2 ** 31)


def _bit_transpose32(words):
    a = list(words)
    j, m = 16, 0x0000FFFF
    while j:
        k = 0
        while k < 32:
            t = (a[k] ^ lax.shift_right_logical(a[k + j], jnp.int32(j))) & jnp.int32(_as_i32(m))
            a[k] = a[k] ^ t
            a[k + j] = a[k + j] ^ (t << j)
            k = (k + j + 1) & ~j
        j >>= 1
        m = (m ^ (m << j)) & 0xFFFFFFFF
    return a[::-1]


def _as_i32(v):
    return v - (1 << 32) if v >= (1 << 31) else v


def _slab_reduce(x, op):
    parts = [x[j * 8:(j + 1) * 8] for j in range(x.shape[0] // 8)]
    while len(parts) > 1:
        nxt = [op(parts[j], parts[j + 1]) for j in range(0, len(parts) - 1, 2)]
        if len(parts) % 2:
            nxt.append(parts[-1])
        parts = nxt
    return parts[0]


def _dsa_kernel(n_sel, qa_ref, qi_ref, wi_ref, ka_ref, ki_ref, va_ref, o_ref,
                score_ref, planes_ref, alive_ref, m_ref, l_ref, acc_ref):
    i = pl.program_id(1)
    qb, kc = DSA_QB, DSA_KC
    pairs = range(N_HEADS // 2)
    va_pieces = kc // ROW_TILE
    n_chunks = (i * qb + qb + kc - 1) // kc
    t_pos = i * qb + lax.broadcasted_iota(I32, (1, qb), 1)
    limit = (t_pos // CHUNK + 1) * CHUNK
    kio = lax.broadcasted_iota(I32, (kc, qb), 0)

    def pair_rows(ref, p):
        return ref[2 * p:2 * p + 2].reshape(2 * qb, HEAD_DIM)

    def twice(x):
        return jnp.concatenate([x, x], axis=1)

    w = wi_ref[...] * (N_HEADS ** -0.5)
    w2 = [jnp.concatenate([w[2 * p:2 * p + 1], w[2 * p + 1:2 * p + 2]], axis=1) for p in pairs]
    q_idx = [pair_rows(qi_ref, p) for p in pairs]
    groups_per_chunk = kc // (8 * 32)
    n_groups = planes_ref.shape[1]

    def idx_body(c, carry):
        start = pl.multiple_of(c * kc, kc)
        keys = ki_ref[pl.ds(start, kc), :]
        tot = None
        for p in pairs:
            t = jnp.maximum(_nt_dot(keys, q_idx[p]), 0.0) * w2[p]
            tot = t if tot is None else tot + t
        sc = jnp.where(start + kio < limit, tot[:, :qb] + tot[:, qb:], -jnp.inf)
        score_ref[c] = sc
        okey = _float_to_key(sc)
        for g in range(groups_per_chunk):
            slabs = [okey[(g * 32 + j) * 8:(g * 32 + j + 1) * 8] for j in range(32)]
            planes = _bit_transpose32(slabs)
            for bit in range(32):
                planes_ref[bit, groups_per_chunk * c + g] = planes[bit]
        return carry

    lax.fori_loop(0, n_chunks, idx_body, 0)

    def count(pred_fn):
        def body(c, acc):
            return acc + _slab_reduce(pred_fn(c).astype(F32), jnp.add)
        acc = lax.fori_loop(0, n_chunks, body, jnp.zeros((8, qb), F32))
        return jnp.sum(acc, axis=0, keepdims=True)

    def kth_largest(val_fn):
        def body(it, carry):
            u, n_ge = carry
            trial = u | (jnp.int32(1) << (31 - it))
            cand = _key_to_float(trial)
            cnt = count(lambda c: val_fn(c) >= cand)
            take = cnt >= n_sel
            return jnp.where(take, trial, u), jnp.where(take, cnt, n_ge)
        u, n_ge = lax.fori_loop(0, 32, body,
                                (jnp.zeros((1, qb), I32), jnp.full((1, qb), n_sel, F32)))
        return _key_to_float(u), n_ge

    def kth_largest_key():
        used = groups_per_chunk * n_chunks
        for g in range(n_groups):
            @pl.when(g >= used)
            def _():
                for bit in range(32):
                    planes_ref[bit, g] = jnp.zeros((8, qb), I32)
            alive_ref[g] = jnp.where(g < used, jnp.int32(-1), jnp.int32(0)) + jnp.zeros((8, qb), I32)

        def body(it, carry):
            u, want = carry
            bit = 31 - it
            cnt8 = jnp.zeros((8, qb), I32)
            for g in range(n_groups):
                cnt8 = cnt8 + lax.population_count(alive_ref[g] & planes_ref[bit, g])
            cnt = jnp.sum(cnt8, axis=0, keepdims=True)
            take = cnt >= want
            flip = jnp.where(take, jnp.int32(0), jnp.int32(-1))
            for g in range(n_groups):
                alive_ref[g] = alive_ref[g] & (planes_ref[bit, g] ^ flip)
            return (jnp.where(take, u | (jnp.int32(1) << bit), u),
                    jnp.where(take, want, want - cnt))
        u, _ = lax.fori_loop(0, 32, body,
                             (jnp.zeros((1, qb), I32), jnp.full((1, qb), n_sel, I32)))
        return _key_to_float(u)

    select_all = limit <= n_sel
    thr = kth_largest_key()
    n_ge = count(lambda c: score_ref[c] >= thr)
    n_gt = count(lambda c: score_ref[c] > thr)
    settled = select_all | ((n_gt < n_sel) & (n_ge >= n_sel))
    thr, n_ge = lax.cond(jnp.min(jnp.where(settled, 1, 0)) > 0,
                         lambda: (thr, n_ge), lambda: kth_largest(lambda c: score_ref[c]))
    thr = jnp.where(select_all, -jnp.inf, thr)

    def to_excess(c, carry):
        s = score_ref[c]
        above = jnp.where(s > thr, jnp.maximum(s - thr, TINY), 0.0)
        score_ref[c] = jnp.where(s >= thr, above, -jnp.inf)
        return carry
    lax.fori_loop(0, n_chunks, to_excess, 0)

    def refine():
        n_above = count(lambda c: score_ref[c] > 0.0)
        crowded = jnp.max(jnp.where((n_above >= n_sel) & jnp.logical_not(select_all), 1, 0)) > 0

        def split():
            t2, _ = kth_largest(lambda c: score_ref[c])
            return t2, n_sel - count(lambda c: score_ref[c] > t2)
        thr2, need = lax.cond(crowded, split, lambda: (jnp.zeros((1, qb), F32), n_sel - n_above))

        def tie_body(it, j):
            trial = j | (jnp.int32(1) << (12 - it))
            g = count(lambda c: (score_ref[c] == thr2) & (c * kc + kio < trial))
            return jnp.where(g <= need, trial, j)
        return thr2, lax.fori_loop(0, 13, tie_body, jnp.zeros((1, qb), I32))

    tied = jnp.max(jnp.where((n_ge > n_sel) & jnp.logical_not(select_all), 1, 0)) > 0
    thr2, j_lim = lax.cond(tied, refine, lambda: (jnp.zeros((1, qb), F32),
                                                  jnp.full((1, qb), 2 ** 30, I32)))
    thr2 = jnp.where(select_all, 0.0, thr2)
    j_lim = jnp.where(select_all, jnp.int32(2 ** 30), j_lim)

    q_att = [pair_rows(qa_ref, p) for p in pairs]
    m_ref[...] = jnp.full_like(m_ref, NEG_BIG)
    l_ref[...] = jnp.zeros_like(l_ref)
    acc_ref[...] = jnp.zeros_like(acc_ref)

    def att_body(c, carry):
        start = pl.multiple_of(c * kc, kc)
        kpos = start + kio
        ex = score_ref[c]
        sel = (kpos < limit) & ((ex > thr2) | ((ex == thr2) & (kpos < j_lim)))
        bias = twice(jnp.where(sel, 0.0, NEG_BIG))
        keys = ka_ref[pl.ds(start, kc), :]
        s = [_nt_dot(keys, q_att[p]) + bias for p in pairs]
        m_old = [m_ref[p] for p in pairs]
        m_new = [jnp.maximum(m_old[p], jnp.max(_slab_reduce(s[p], jnp.maximum), axis=0,
                                               keepdims=True)) for p in pairs]
        alpha = [jnp.exp(m_old[p] - m_new[p]) for p in pairs]
        pe = [jnp.exp(s[p] - m_new[p]) for p in pairs]
        pv = []
        for p in pairs:
            pb = pe[p].astype(BF16)
            tot = None
            for v in range(va_pieces):
                part = jnp.dot(va_ref[va_pieces * c + v], pb[v * ROW_TILE:(v + 1) * ROW_TILE],
                               preferred_element_type=F32)
                tot = part if tot is None else tot + part
            pv.append(tot)
        for p in pairs:
            l_ref[p] = alpha[p] * l_ref[p] + jnp.sum(_slab_reduce(pe[p], jnp.add), axis=0,
                                                     keepdims=True)
            acc_ref[p] = acc_ref[p] * alpha[p] + pv[p]
            m_ref[p] = m_new[p]
        return carry

    lax.fori_loop(0, n_chunks, att_body, 0)
    for p in pairs:
        out = acc_ref[p] / l_ref[p]
        both = jnp.concatenate([out[:, :qb], out[:, qb:]], axis=0)
        o_ref[:, 2 * p * HEAD_DIM:2 * (p + 1) * HEAD_DIM] = both.T.astype(BF16)


def _dsa(qa, qi, wi, ka, ki, va, batch, seq):
    n = batch * seq
    nqb = seq // DSA_QB
    n_sel = min(MAX_SELECTED_KEYS, seq // 4)
    head_spec = pl.BlockSpec((N_HEADS, DSA_QB, HEAD_DIM), lambda b, i: (0, b * nqb + i, 0))
    kv_spec = pl.BlockSpec((seq, HEAD_DIM), lambda b, i: (b, 0))
    pair_stat = pltpu.VMEM((N_HEADS // 2, 1, 2 * DSA_QB), F32)
    return pl.pallas_call(
        functools.partial(_dsa_kernel, n_sel),
        grid=(batch, nqb),
        in_specs=[head_spec, head_spec,
                  pl.BlockSpec((N_HEADS, DSA_QB), lambda b, i: (0, b * nqb + i)),
                  kv_spec, kv_spec,
                  pl.BlockSpec((seq // ROW_TILE, HEAD_DIM, ROW_TILE), lambda b, i: (b, 0, 0))],
        out_specs=pl.BlockSpec((DSA_QB, N_HEADS * HEAD_DIM), lambda b, i: (b * nqb + i, 0)),
        out_shape=jax.ShapeDtypeStruct((n, N_HEADS * HEAD_DIM), BF16),
        scratch_shapes=[pltpu.VMEM((seq // DSA_KC, DSA_KC, DSA_QB), F32),
                        pltpu.VMEM((32, seq // 256, 8, DSA_QB), I32),
                        pltpu.VMEM((seq // 256, 8, DSA_QB), I32), pair_stat, pair_stat,
                        pltpu.VMEM((N_HEADS // 2, HEAD_DIM, 2 * DSA_QB), F32)],
        compiler_params=pltpu.CompilerParams(vmem_limit_bytes=VMEM_LIMIT),
        name="dsa",
    )(qa, qi, wi, ka, ki, va)


def _sb_kernel(q_ref, k_ref, v_ref, o_ref):
    i = pl.program_id(2)
    t = SB_T
    row = lax.broadcasted_iota(I32, (t, t), 0)
    col = lax.broadcasted_iota(I32, (t, t), 1)
    tri = jnp.where(row > col, 1.0, 0.0).astype(BF16)
    causal = col < row

    def block(j, state, diag):
        heads = range(SB_HEADS)
        start = pl.multiple_of(j * t, t)
        zs = [_nt_dot(q_ref[hd], k_ref[hd, pl.ds(start, t), :]) for hd in heads]
        lks = []
        for z in zs:
            lk = -(jnp.maximum(z, 0.0) + jnp.log(1.0 + jnp.exp(-jnp.abs(z))))
            lks.append(jnp.where(causal, lk, 0.0) if diag else lk)
        parts = [_split_bf16(lk) for lk in lks]
        laters = [jnp.dot(hi, tri, preferred_element_type=F32)
                  + jnp.dot(lo, tri, preferred_element_type=F32)
                  for hi, lo in parts]
        out = []
        for hd in heads:
            carry, acc = state[hd]
            a = jnp.exp(zs[hd] + lks[hd] + laters[hd] + carry)
            if diag:
                a = jnp.where(causal, a, 0.0)
            acc = acc + jnp.dot(a.astype(BF16), v_ref[hd, pl.ds(start, t), :],
                                preferred_element_type=F32)
            out.append((carry + jnp.sum(lks[hd], axis=-1, keepdims=True), acc))
        return tuple(out)

    state = tuple((jnp.zeros((t, 1), F32), jnp.zeros((t, LANES), F32))
                  for _ in range(SB_HEADS))
    state = block(i, state, True)

    def live(loop):
        jj, st = loop
        top = st[0][0]
        for hd in range(1, SB_HEADS):
            top = jnp.maximum(top, st[hd][0])
        return (jj < i) & (jnp.max(top) > SB_LOG_ZERO)

    _, state = lax.while_loop(live, lambda loop: (loop[0] + 1, block(i - 1 - loop[0], loop[1], False)),
                              (jnp.int32(0), state))
    for p in range(SB_HEADS // 2):
        pair = state[2 * p][1] + state[2 * p + 1][1]
        o_ref[:, p * LANES:(p + 1) * LANES] = pair.astype(BF16)


def _sb(qb, kb, vb, batch, seq):
    n = batch * seq
    nq = seq // SB_T
    assert SB_HEADS == N_HEADS
    q_spec = pl.BlockSpec((SB_HEADS, SB_T, HEAD_DIM), lambda b, h, i: (h, b * nq + i, 0))
    k_spec = pl.BlockSpec((SB_HEADS, seq, HEAD_DIM), lambda b, h, i: (h, b, 0))
    v_spec = pl.BlockSpec((SB_HEADS, seq, LANES), lambda b, h, i: (h, b, 0))
    return pl.pallas_call(
        _sb_kernel,
        grid=(batch, N_HEADS // SB_HEADS, nq),
        in_specs=[q_spec, k_spec, v_spec],
        out_specs=pl.BlockSpec((SB_T, N_HEADS * HEAD_DIM), lambda b, h, i: (b * nq + i, 0)),
        out_shape=jax.ShapeDtypeStruct((n, N_HEADS * HEAD_DIM), BF16),
        compiler_params=pltpu.CompilerParams(vmem_limit_bytes=VMEM_LIMIT),
        name="stickbreak",
    )(qb, kb, vb)


def _outproj_kernel(x_ref, oa_ref, ob_ref, ga_ref, gb_ref, wpa_ref, wpb_ref, wo_ref,
                    gffn_ref, wr_hi_ref, wr_lo_ref, br_ref,
                    h_ref, hn_ref, idx_ref, gate_ref, rank_ref, cnt_ref, carry_ref):
    step = pl.program_id(0)
    tm = x_ref.shape[0]

    @pl.when(step == 0)
    def _():
        carry_ref[...] = jnp.zeros_like(carry_ref)

    pa = jnp.dot(oa_ref[...], wpa_ref[...], preferred_element_type=F32)
    pb = jnp.dot(ob_ref[...], wpb_ref[...], preferred_element_type=F32)
    merged = jax.nn.sigmoid(ga_ref[...]) * pa + jax.nn.sigmoid(gb_ref[...]) * pb
    h = x_ref[...] + jnp.dot(merged.astype(BF16), wo_ref[...], preferred_element_type=F32)
    h_ref[...] = h
    ms = jnp.mean(h * h, axis=-1, keepdims=True)
    hn = h * lax.rsqrt(ms + RMS_EPS) * gffn_ref[...]
    hn_ref[...] = hn

    hn_hi, hn_lo = _split_bf16(hn)
    logits = (jnp.dot(hn_hi, wr_hi_ref[...], preferred_element_type=F32)
              + jnp.dot(hn_lo, wr_hi_ref[...], preferred_element_type=F32)
              + jnp.dot(hn_hi, wr_lo_ref[...], preferred_element_type=F32)) + br_ref[...]

    lane = lax.broadcasted_iota(I32, (tm, N_EXPERTS), 1)
    lane_k = lax.broadcasted_iota(I32, (tm, TOP_K), 1)
    vals = logits
    picks, top_vals, top_idx = [], [], []
    for _ in range(TOP_K):
        mx = jnp.max(vals, axis=-1, keepdims=True)
        first = jnp.min(jnp.where(vals == mx, lane, N_EXPERTS), axis=-1, keepdims=True)
        pick = lane == first
        picks.append(pick)
        top_vals.append(mx)
        top_idx.append(first)
        vals = jnp.where(pick, -jnp.inf, vals)
    exps = [jnp.exp(v - top_vals[0]) for v in top_vals]
    denom = exps[0] + exps[1] + exps[2] + exps[3]

    any_pick = picks[0] | picks[1] | picks[2] | picks[3]
    onehot = jnp.where(any_pick, 1.0, 0.0)
    r_i = lax.broadcasted_iota(I32, (tm, tm), 0)
    c_i = lax.broadcasted_iota(I32, (tm, tm), 1)
    lower = jnp.where(c_i < r_i, 1.0, 0.0).astype(BF16)
    pos = jnp.dot(lower, onehot.astype(BF16), preferred_element_type=F32) + carry_ref[...]
    carry_ref[...] = carry_ref[...] + jnp.sum(onehot, axis=0, keepdims=True)
    cnt_ref[...] = carry_ref[...].astype(I32)

    idx4 = jnp.zeros((tm, TOP_K), I32)
    gate4 = jnp.zeros((tm, TOP_K), F32)
    rank4 = jnp.zeros((tm, TOP_K), I32)
    for k in range(TOP_K):
        rk = jnp.sum(jnp.where(picks[k], pos, 0.0), axis=-1, keepdims=True).astype(I32)
        idx4 = jnp.where(lane_k == k, top_idx[k], idx4)
        gate4 = jnp.where(lane_k == k, exps[k] / denom, gate4)
        rank4 = jnp.where(lane_k == k, rk, rank4)
    idx_ref[...] = idx4
    gate_ref[...] = gate4
    rank_ref[...] = rank4


def _outproj(x2, oa, ob, ga, gb, w_proj_a, w_proj_b, w_out, norm_ffn_g, w_router, b_router):
    n = x2.shape[0]
    tm = ROW_TILE
    width = N_HEADS * HEAD_DIM
    wpa = w_proj_a.astype(BF16)
    wpb = w_proj_b.astype(BF16)
    wr_hi = w_router.astype(BF16)
    wr_lo = (w_router - wr_hi.astype(F32)).astype(BF16)
    full = lambda shape: pl.BlockSpec(shape, lambda i: (0,) * len(shape))
    row = lambda width: pl.BlockSpec((tm, width), lambda i: (i, 0))
    return pl.pallas_call(
        _outproj_kernel,
        grid=(n // tm,),
        in_specs=[row(D_MODEL), row(width), row(width), row(D_MODEL), row(D_MODEL),
                  full((width, D_MODEL)), full((width, D_MODEL)),
                  full((D_MODEL, D_MODEL)), full((1, D_MODEL)),
                  full((D_MODEL, N_EXPERTS)), full((D_MODEL, N_EXPERTS)), full((1, N_EXPERTS))],
        out_specs=[row(D_MODEL), row(D_MODEL), row(TOP_K), row(TOP_K), row(TOP_K),
                   full((1, N_EXPERTS))],
        out_shape=[jax.ShapeDtypeStruct((n, D_MODEL), F32), jax.ShapeDtypeStruct((n, D_MODEL), F32),
                   jax.ShapeDtypeStruct((n, TOP_K), I32), jax.ShapeDtypeStruct((n, TOP_K), F32),
                   jax.ShapeDtypeStruct((n, TOP_K), I32), jax.ShapeDtypeStruct((1, N_EXPERTS), I32)],
        scratch_shapes=[pltpu.VMEM((1, N_EXPERTS), F32)],
        compiler_params=pltpu.CompilerParams(vmem_limit_bytes=VMEM_LIMIT,
                                             dimension_semantics=("arbitrary",)),
        name="outproj_router",
    )(x2, oa, ob, ga, gb, wpa, wpb, w_out.astype(BF16), norm_ffn_g[None, :],
      wr_hi, wr_lo, b_router[None, :])


def _dispatch_kernel(dest_ref, fill_lo_ref, fill_hi_ref, hn_ref, xs_ref, zero_ref, sem, zsem):
    s = pl.program_id(0)
    dt = hn_ref.shape[0]

    def tok(tt, carry):
        t = s * dt + tt
        for k in range(TOP_K):
            d = dest_ref[t * TOP_K + k]
            pltpu.make_async_copy(hn_ref.at[pl.ds(tt, 1)], xs_ref.at[pl.ds(d, 1)],
                                  sem).start(priority=k % 2)
        return carry
    lax.fori_loop(0, dt, tok, 0, unroll=4)

    @pl.when(s == 0)
    def _():
        zero_ref[...] = jnp.zeros_like(zero_ref)

        def expert(e, total):
            lo, hi = fill_lo_ref[e], fill_hi_ref[e]

            def fill(r, carry):
                pltpu.make_async_copy(zero_ref, xs_ref.at[pl.ds(r, 1)], zsem).start()
                return carry
            lax.fori_loop(lo, hi, fill, 0)
            return total + (hi - lo)
        total = lax.fori_loop(0, fill_lo_ref.shape[0], expert, jnp.int32(0))

        def drain(r, carry):
            pltpu.make_async_copy(zero_ref, xs_ref.at[pl.ds(0, 1)], zsem).wait()
            return carry
        lax.fori_loop(0, total, drain, 0)

    for _ in range(TOP_K):
        pltpu.make_async_copy(hn_ref, xs_ref.at[pl.ds(0, dt)], sem).wait()


def _dispatch(dest_flat, fill_lo, fill_hi, hn, rows):
    n = hn.shape[0]
    dt = DISPATCH_T
    return pl.pallas_call(
        _dispatch_kernel,
        grid_spec=pltpu.PrefetchScalarGridSpec(
            num_scalar_prefetch=3,
            grid=(n // dt,),
            in_specs=[pl.BlockSpec((dt, D_MODEL), lambda i, d, lo, hi: (i, 0))],
            out_specs=pl.BlockSpec(memory_space=pl.ANY),
            scratch_shapes=[pltpu.VMEM((1, D_MODEL), F32),
                            pltpu.SemaphoreType.DMA(()), pltpu.SemaphoreType.DMA(())]),
        out_shape=jax.ShapeDtypeStruct((rows, D_MODEL), F32),
        compiler_params=pltpu.CompilerParams(vmem_limit_bytes=VMEM_LIMIT,
                                             dimension_semantics=("arbitrary",),
                                             has_side_effects=True),
        name="dispatch",
    )(dest_flat, fill_lo, fill_hi, hn)


def _expert_kernel(be_ref, nu_ref, x_ref, wgu_ref, bgu_ref, wd_ref, bd_ref, y_ref,
                   wgu_bf, wd_bf):
    b = pl.program_id(0)
    n_used = nu_ref[0]
    prev = be_ref[jnp.maximum(b - 1, 0)]

    @pl.when((b < n_used) & ((b == 0) | (be_ref[b] != prev)))
    def _():
        wgu_bf[...] = wgu_ref[0].astype(BF16)
        wd_bf[...] = wd_ref[0].astype(BF16)

    @pl.when(b < n_used)
    def _():
        xb = x_ref[...].astype(BF16)
        hgu = jnp.dot(xb, wgu_bf[...], preferred_element_type=F32) + bgu_ref[0]
        gate = jnp.minimum(hgu[:, :D_EXPERT], SWIGLU_LIMIT)
        up = jnp.clip(hgu[:, D_EXPERT:], -SWIGLU_LIMIT, SWIGLU_LIMIT)
        act = (up + 1.0) * gate * jax.nn.sigmoid(SWIGLU_ALPHA * gate)
        y_ref[...] = (jnp.dot(act.astype(BF16), wd_bf[...], preferred_element_type=F32)
                      + bd_ref[0])

    @pl.when(b >= n_used)
    def _():
        y_ref[...] = jnp.zeros_like(y_ref)


def _experts(block_expert, n_used, xs, w_gu, b_gu, w_down, b_down):
    rows = xs.shape[0]
    n_blocks = rows // EXPERT_BLOCK

    def blk(b, be, nu):
        return jnp.minimum(b, nu[0] - 1)

    w_spec = lambda d1, d2: pl.BlockSpec((1, d1, d2), lambda b, be, nu: (be[blk(b, be, nu)], 0, 0))
    return pl.pallas_call(
        _expert_kernel,
        grid_spec=pltpu.PrefetchScalarGridSpec(
            num_scalar_prefetch=2,
            grid=(n_blocks,),
            in_specs=[pl.BlockSpec((EXPERT_BLOCK, D_MODEL), lambda b, be, nu: (blk(b, be, nu), 0)),
                      w_spec(D_MODEL, 2 * D_EXPERT), w_spec(1, 2 * D_EXPERT),
                      w_spec(D_EXPERT, D_MODEL), w_spec(1, D_MODEL)],
            out_specs=pl.BlockSpec((EXPERT_BLOCK, D_MODEL), lambda b, be, nu: (b, 0)),
            scratch_shapes=[pltpu.VMEM((D_MODEL, 2 * D_EXPERT), BF16),
                            pltpu.VMEM((D_EXPERT, D_MODEL), BF16)]),
        out_shape=jax.ShapeDtypeStruct((rows, D_MODEL), F32),
        compiler_params=pltpu.CompilerParams(vmem_limit_bytes=VMEM_LIMIT,
                                             dimension_semantics=("arbitrary",)),
        name="experts",
    )(block_expert, n_used, xs, w_gu, b_gu[:, None, :], w_down, b_down[:, None, :])


def _combine_kernel(dest_ref, h_ref, gate_ref, y_ref, o_ref, buf_ref, sems):
    s = pl.program_id(0)
    n_steps = pl.num_programs(0)
    ct = COMBINE_T

    def issue(step, slot):
        def tok(tt, carry):
            t = step * ct + tt
            for k in range(TOP_K):
                d = dest_ref[t * TOP_K + k]
                pltpu.make_async_copy(y_ref.at[pl.ds(d, 1)], buf_ref.at[slot, k, pl.ds(tt, 1)],
                                      sems.at[slot]).start(priority=k % 2)
            return carry
        lax.fori_loop(0, ct, tok, 0, unroll=4)

    @pl.when(s == 0)
    def _():
        issue(0, 0)

    for parity in range(2):
        @pl.when((s + 1 < n_steps) & ((s + 1) % 2 == parity))
        def _():
            issue(s + 1, parity)

    slot = s % 2
    for k in range(TOP_K):
        pltpu.make_async_copy(y_ref.at[pl.ds(0, ct)], buf_ref.at[slot, k], sems.at[slot]).wait()
    g = gate_ref[...]
    out = h_ref[...]
    for k in range(TOP_K):
        out = out + g[:, k:k + 1] * buf_ref[slot, k]
    o_ref[...] = out


def _combine(dest_flat, h, gates, y):
    n = h.shape[0]
    ct = COMBINE_T
    return pl.pallas_call(
        _combine_kernel,
        grid_spec=pltpu.PrefetchScalarGridSpec(
            num_scalar_prefetch=1,
            grid=(n // ct,),
            in_specs=[pl.BlockSpec((ct, D_MODEL), lambda i, d: (i, 0)),
                      pl.BlockSpec((ct, TOP_K), lambda i, d: (i, 0)),
                      pl.BlockSpec(memory_space=pl.ANY)],
            out_specs=pl.BlockSpec((ct, D_MODEL), lambda i, d: (i, 0)),
            scratch_shapes=[pltpu.VMEM((2, TOP_K, ct, D_MODEL), F32),
                            pltpu.SemaphoreType.DMA((2,))]),
        out_shape=jax.ShapeDtypeStruct((n, D_MODEL), F32),
        compiler_params=pltpu.CompilerParams(vmem_limit_bytes=VMEM_LIMIT,
                                             dimension_semantics=("arbitrary",)),
        name="combine",
    )(dest_flat, h, gates, y)


def _moe(h, hn, top_idx, gates, rank, counts, w_gate_up, b_gate_up, w_down, b_down):
    n = h.shape[0]
    m = n * TOP_K
    n_blocks = (m + N_EXPERTS * (EXPERT_BLOCK - 1) + EXPERT_BLOCK - 1) // EXPERT_BLOCK
    rows = n_blocks * EXPERT_BLOCK
    counts = counts[0]
    padded = ((counts + EXPERT_BLOCK - 1) // EXPERT_BLOCK) * EXPERT_BLOCK
    pend = jnp.cumsum(padded).astype(I32)
    pstart = pend - padded
    dest = (pstart[top_idx] + rank).reshape(m).astype(I32)
    block_start = jnp.arange(n_blocks, dtype=I32) * EXPERT_BLOCK
    block_expert = jnp.minimum(jnp.sum(pend[None, :] <= block_start[:, None], axis=1),
                               N_EXPERTS - 1).astype(I32)
    n_used = (pend[-1:] // EXPERT_BLOCK).astype(I32)
    fill_lo = jnp.concatenate([pstart + counts, pend[-1:]]).astype(I32)
    fill_hi = jnp.concatenate([pend, jnp.full((1,), rows, I32)])
    xs = _dispatch(dest, fill_lo, fill_hi, hn, rows)
    y = _experts(block_expert, n_used, xs, w_gate_up, b_gate_up, w_down, b_down)
    return _combine(dest, h, gates, y)


def kernel(x, norm_mix_g, w_in, q_norm_g, k_norm_g, w_proj_a, w_proj_b, w_out, norm_ffn_g,
           w_router, b_router, w_gate_up, b_gate_up, w_down, b_down):
    batch, seq, d = x.shape
    h = x.reshape(batch * seq, d)
    for l in range(norm_mix_g.shape[0]):
        (qa, qi, qb, kb, vb, ka, ki, va, wi, ga, gb) = _inproj(
            h, norm_mix_g[l], w_in[l], q_norm_g[l], k_norm_g[l], seq)
        oa = _dsa(qa, qi, wi, ka, ki, va, batch, seq)
        ob = _sb(qb, kb, vb, batch, seq)
        h_mid, hn, top_idx, gates, rank, counts = _outproj(
            h, oa, ob, ga, gb, w_proj_a[l], w_proj_b[l], w_out[l], norm_ffn_g[l],
            w_router[l], b_router[l])
        h = _moe(h_mid, hn, top_idx, gates, rank, counts,
                 w_gate_up[l], b_gate_up[l], w_down[l], b_down[l])
    return h.reshape(batch, seq, d)
```

```python
import functools

import jax
import jax.numpy as jnp
from jax import lax
from jax.experimental import pallas as pl
from jax.experimental.pallas import tpu as pltpu

F32 = jnp.float32
BF16 = jnp.bfloat16
I32 = jnp.int32

D_MODEL = 1024
CHUNK = 64
HEAD_DIM = 64
N_HEADS = 8
MAX_SELECTED_KEYS = 256
ROT_DIM = HEAD_DIM // 4
ROPE_THETA = 500000.0
N_EXPERTS = 32
TOP_K = 4
D_EXPERT = D_MODEL
SWIGLU_LIMIT = 7.0
SWIGLU_ALPHA = 1.702
EXPERT_BLOCK = 256
RMS_EPS = 1e-6

LANES = 128
VMEM_LIMIT = 56 * 1024 * 1024

ROW_TILE = 256
DSA_QB = 256
DSA_KC = 512
SB_T = 256
SB_HEADS = 8
COMBINE_T = 128
DISPATCH_T = 512

NEG_BIG = -1e30
TINY = 1e-30
SB_LOG_ZERO = -120.0


def _nt_dot(a, b):
    return lax.dot_general(a, b, (((1,), (1,)), ((), ())), preferred_element_type=F32)


def _split_bf16(x):
    hi = x.astype(BF16)
    lo = (x - hi.astype(F32)).astype(BF16)
    return hi, lo


_OFF_QA, _OFF_QI, _OFF_QB, _OFF_KB, _OFF_VB = 0, 512, 1024, 1536, 2048
_OFF_KK, _OFF_VW, _OFF_GA, _OFF_GB, _W_COLS = 2560, 2688, 2816, 3840, 4864


def _rope128(y, c, s1, s2):
    return y * c + pltpu.roll(y, 8, 1) * s1 + pltpu.roll(y, LANES - 8, 1) * s2


def _head_rms_inv(x, norm_lo, norm_hi):
    lane = lax.broadcasted_iota(I32, x.shape, 1)
    lo = lane < HEAD_DIM
    sq = x * x
    one = jnp.ones((x.shape[0], 1), F32)
    inv_lo = one
    inv_hi = one
    if norm_lo:
        s = jnp.sum(jnp.where(lo, sq, 0.0), axis=-1, keepdims=True)
        inv_lo = lax.rsqrt(s * (1.0 / HEAD_DIM) + RMS_EPS)
    if norm_hi:
        s = jnp.sum(jnp.where(lo, 0.0, sq), axis=-1, keepdims=True)
        inv_hi = lax.rsqrt(s * (1.0 / HEAD_DIM) + RMS_EPS)
    return jnp.where(lo, inv_lo, inv_hi)


def _inproj_kernel(x_ref, gmix_ref, w_ref, wvt_ref, qg_ref, kg_ref, c_ref, s1_ref, s2_ref,
                   qa_ref, qi_ref, qb_ref, kb_ref, vb_ref, ka_ref, ki_ref, va_ref,
                   wi_ref, ga_ref, gb_ref):
    x = x_ref[...]
    ms = jnp.mean(x * x, axis=-1, keepdims=True)
    xn = (x * lax.rsqrt(ms + RMS_EPS) * gmix_ref[...]).astype(BF16)

    def proj(off, width):
        return jnp.dot(xn, w_ref[:, off:off + width], preferred_element_type=F32)

    c, s1, s2 = c_ref[...], s1_ref[...], s2_ref[...]
    scale = HEAD_DIM ** -0.5

    def put_heads(dst_ref, pair, val):
        dst_ref[2 * pair] = val[:, :HEAD_DIM].astype(BF16)
        dst_ref[2 * pair + 1] = val[:, HEAD_DIM:].astype(BF16)

    acc = proj(_OFF_QA, 512)
    for p in range(4):
        blk = acc[:, p * LANES:(p + 1) * LANES]
        y = blk * _head_rms_inv(blk, True, True) * qg_ref[...]
        put_heads(qa_ref, p, _rope128(y, c, s1, s2) * scale)

    acc = proj(_OFF_QI, 512)
    for p in range(4):
        blk = acc[:, p * LANES:(p + 1) * LANES]
        put_heads(qi_ref, p, _rope128(blk, c, s1, s2) * scale)

    acc = proj(_OFF_QB, 512)
    for p in range(4):
        put_heads(qb_ref, p, acc[:, p * LANES:(p + 1) * LANES] * scale)
    acc = proj(_OFF_KB, 512)
    for p in range(4):
        put_heads(kb_ref, p, acc[:, p * LANES:(p + 1) * LANES])
    acc = proj(_OFF_VB, 512)
    low_half = lax.broadcasted_iota(I32, (x.shape[0], LANES), 1) < HEAD_DIM
    for p in range(4):
        blk = acc[:, p * LANES:(p + 1) * LANES]
        vb_ref[2 * p] = jnp.where(low_half, blk, 0.0).astype(BF16)
        vb_ref[2 * p + 1] = jnp.where(low_half, 0.0, blk).astype(BF16)

    kk = proj(_OFF_KK, LANES)
    y = kk * _head_rms_inv(kk, True, False) * kg_ref[...]
    y = _rope128(y, c, s1, s2)
    ka_ref[...] = y[:, :HEAD_DIM].astype(BF16)
    ki_ref[...] = y[:, HEAD_DIM:].astype(BF16)

    vw_t = _nt_dot(wvt_ref[...], xn)
    va_ref[0] = vw_t[:HEAD_DIM].astype(BF16)
    wi_ref[...] = vw_t[HEAD_DIM:HEAD_DIM + N_HEADS]

    ga_ref[...] = proj(_OFF_GA, D_MODEL)
    gb_ref[...] = proj(_OFF_GB, D_MODEL)


def _rope_tables(seq):
    pos = jnp.arange(seq, dtype=F32)
    inv_freq = ROPE_THETA ** (-jnp.arange(0, ROT_DIM, 2, dtype=F32) / ROT_DIM)
    ang = pos[:, None] * inv_freq[None, :]
    cos, sin = jnp.cos(ang), jnp.sin(ang)
    half = ROT_DIM // 2
    zeros = jnp.zeros((seq, HEAD_DIM - ROT_DIM), F32)
    zh = jnp.zeros((seq, half), F32)
    c64 = jnp.concatenate([cos, cos, zeros + 1.0], axis=1)
    s1_64 = jnp.concatenate([zh, sin, zeros], axis=1)
    s2_64 = jnp.concatenate([-sin, zh, zeros], axis=1)
    tile2 = lambda t: jnp.concatenate([t, t], axis=1)
    return tile2(c64), tile2(s1_64), tile2(s2_64)


def _inproj(x2, norm_g, w_in, q_norm_g, k_norm_g, seq):
    n = x2.shape[0]
    w = w_in
    sl = lambda a, b: w[:, a:b]
    o = [0, 512, 576, 640, 1152, 1216, 1224, 1736, 2248, 2760, 3784, 4808]
    pad = jnp.zeros((D_MODEL, LANES - HEAD_DIM - N_HEADS), w.dtype)
    w_all = jnp.concatenate([
        sl(o[0], o[1]), sl(o[3], o[4]), sl(o[6], o[7]), sl(o[7], o[8]), sl(o[8], o[9]),
        sl(o[1], o[2]), sl(o[4], o[5]),
        sl(o[2], o[3]), sl(o[5], o[6]), pad,
        sl(o[9], o[10]), sl(o[10], o[11])], axis=1).astype(BF16)
    assert w_all.shape[1] == _W_COLS
    qg = jnp.concatenate([q_norm_g, q_norm_g])[None, :]
    kg = jnp.concatenate([k_norm_g, jnp.ones_like(k_norm_g)])[None, :]
    c, s1, s2 = _rope_tables(seq)
    tm = ROW_TILE
    pos_blocks = seq // tm
    full = lambda shape: pl.BlockSpec(shape, lambda i: (0,) * len(shape))
    head_out = jax.ShapeDtypeStruct((N_HEADS, n, HEAD_DIM), BF16)
    head_spec = pl.BlockSpec((N_HEADS, tm, HEAD_DIM), lambda i: (0, i, 0))
    row64 = jax.ShapeDtypeStruct((n, HEAD_DIM), BF16)
    row64_spec = pl.BlockSpec((tm, HEAD_DIM), lambda i: (i, 0))
    tab_spec = pl.BlockSpec((tm, LANES), lambda i: (i % pos_blocks, 0))
    return pl.pallas_call(
        _inproj_kernel,
        grid=(n // tm,),
        in_specs=[pl.BlockSpec((tm, D_MODEL), lambda i: (i, 0)),
                  full((1, D_MODEL)), full((D_MODEL, _W_COLS)), full((LANES, D_MODEL)),
                  full((1, LANES)), full((1, LANES)), tab_spec, tab_spec, tab_spec],
        out_specs=[head_spec] * 4 + [pl.BlockSpec((N_HEADS, tm, LANES), lambda i: (0, i, 0))]
                  + [row64_spec] * 2
                  + [pl.BlockSpec((1, HEAD_DIM, tm), lambda i: (i, 0, 0)),
                     pl.BlockSpec((N_HEADS, tm), lambda i: (0, i)),
                     pl.BlockSpec((tm, D_MODEL), lambda i: (i, 0)),
                     pl.BlockSpec((tm, D_MODEL), lambda i: (i, 0))],
        out_shape=[head_out] * 4 + [jax.ShapeDtypeStruct((N_HEADS, n, LANES), BF16)]
                  + [row64] * 2
                  + [jax.ShapeDtypeStruct((n // tm, HEAD_DIM, tm), BF16),
                     jax.ShapeDtypeStruct((N_HEADS, n), F32),
                     jax.ShapeDtypeStruct((n, D_MODEL), F32),
                     jax.ShapeDtypeStruct((n, D_MODEL), F32)],
        compiler_params=pltpu.CompilerParams(vmem_limit_bytes=VMEM_LIMIT),
        name="inproj",
    )(x2, norm_g[None, :], w_all, w_all[:, _OFF_VW:_OFF_VW + LANES].T, qg, kg, c, s1, s2)


def _key_to_float(u):
    k = u ^ jnp.int32(-2 ** 31)
    b = k ^ ((k >> 31) & jnp.int32(0x7FFFFFFF))
    return lax.bitcast_convert_type(b, F32)


def _float_to_key(x):
    b = lax.bitcast_convert_type(x, I32)
    return b ^ ((b >> 31) & jnp.int32(0x7FFFFFFF)) ^ jnp.int32(-2 ** 31)


def _bit_transpose32(words):
    a = list(words)
    j, m = 16, 0x0000FFFF
    while j:
        k = 0
        while k < 32:
            t = (a[k] ^ lax.shift_right_logical(a[k + j], jnp.int32(j))) & jnp.int32(_as_i32(m))
            a[k] = a[k] ^ t
            a[k + j] = a[k + j] ^ (t << j)
            k = (k + j + 1) & ~j
        j >>= 1
        m = (m ^ (m << j)) & 0xFFFFFFFF
    return a[::-1]


def _as_i32(v):
    return v - (1 << 32) if v >= (1 << 31) else v


def _slab_reduce(x, op):
    parts = [x[j * 8:(j + 1) * 8] for j in range(x.shape[0] // 8)]
    while len(parts) > 1:
        nxt = [op(parts[j], parts[j + 1]) for j in range(0, len(parts) - 1, 2)]
        if len(parts) % 2:
            nxt.append(parts[-1])
        parts = nxt
    return parts[0]


def _dsa_kernel(n_sel, qa_ref, qi_ref, wi_ref, ka_ref, ki_ref, va_ref, o_ref,
                score_ref, planes_ref, alive_ref, m_ref, l_ref, acc_ref):
    i = pl.program_id(1)
    qb, kc = DSA_QB, DSA_KC
    pairs = range(N_HEADS // 2)
    va_pieces = kc // ROW_TILE
    n_chunks = (i * qb + qb + kc - 1) // kc
    t_pos = i * qb + lax.broadcasted_iota(I32, (1, qb), 1)
    limit = (t_pos // CHUNK + 1) * CHUNK
    kio = lax.broadcasted_iota(I32, (kc, qb), 0)

    def pair_rows(ref, p):
        return ref[2 * p:2 * p + 2].reshape(2 * qb, HEAD_DIM)

    def twice(x):
        return jnp.concatenate([x, x], axis=1)

    w = wi_ref[...] * (N_HEADS ** -0.5)
    w2 = [jnp.concatenate([w[2 * p:2 * p + 1], w[2 * p + 1:2 * p + 2]], axis=1) for p in pairs]
    q_idx = [pair_rows(qi_ref, p) for p in pairs]
    groups_per_chunk = kc // (8 * 32)
    n_groups = planes_ref.shape[1]

    def idx_body(c, carry):
        start = pl.multiple_of(c * kc, kc)
        keys = ki_ref[pl.ds(start, kc), :]
        tot = None
        for p in pairs:
            t = jnp.maximum(_nt_dot(keys, q_idx[p]), 0.0) * w2[p]
            tot = t if tot is None else tot + t
        sc = jnp.where(start + kio < limit, tot[:, :qb] + tot[:, qb:], -jnp.inf)
        score_ref[c] = sc
        okey = _float_to_key(sc)
        for g in range(groups_per_chunk):
            slabs = [okey[(g * 32 + j) * 8:(g * 32 + j + 1) * 8] for j in range(32)]
            planes = _bit_transpose32(slabs)
            for bit in range(32):
                planes_ref[bit, groups_per_chunk * c + g] = planes[bit]
        return carry

    lax.fori_loop(0, n_chunks, idx_body, 0)

    def count(pred_fn):
        def body(c, acc):
            return acc + _slab_reduce(pred_fn(c).astype(F32), jnp.add)
        acc = lax.fori_loop(0, n_chunks, body, jnp.zeros((8, qb), F32))
        return jnp.sum(acc, axis=0, keepdims=True)

    def kth_largest(val_fn):
        def body(it, carry):
            u, n_ge = carry
            trial = u | (jnp.int32(1) << (31 - it))
            cand = _key_to_float(trial)
            cnt = count(lambda c: val_fn(c) >= cand)
            take = cnt >= n_sel
            return jnp.where(take, trial, u), jnp.where(take, cnt, n_ge)
        u, n_ge = lax.fori_loop(0, 32, body,
                                (jnp.zeros((1, qb), I32), jnp.full((1, qb), n_sel, F32)))
        return _key_to_float(u), n_ge

    def kth_largest_key():
        used = groups_per_chunk * n_chunks
        for g in range(n_groups):
            @pl.when(g >= used)
            def _():
                for bit in range(32):
                    planes_ref[bit, g] = jnp.zeros((8, qb), I32)
            alive_ref[g] = jnp.where(g < used, jnp.int32(-1), jnp.int32(0)) + jnp.zeros((8, qb), I32)

        def body(it, carry):
            u, want = carry
            bit = 31 - it
            cnt8 = jnp.zeros((8, qb), I32)
            for g in range(n_groups):
                cnt8 = cnt8 + lax.population_count(alive_ref[g] & planes_ref[bit, g])
            cnt = jnp.sum(cnt8, axis=0, keepdims=True)
            take = cnt >= want
            flip = jnp.where(take, jnp.int32(0), jnp.int32(-1))
            for g in range(n_groups):
                alive_ref[g] = alive_ref[g] & (planes_ref[bit, g] ^ flip)
            return (jnp.where(take, u | (jnp.int32(1) << bit), u),
                    jnp.where(take, want, want - cnt))
        u, _ = lax.fori_loop(0, 32, body,
                             (jnp.zeros((1, qb), I32), jnp.full((1, qb), n_sel, I32)))
        return _key_to_float(u)

    select_all = limit <= n_sel
    thr = kth_largest_key()
    n_ge = count(lambda c: score_ref[c] >= thr)
    n_gt = count(lambda c: score_ref[c] > thr)
    settled = select_all | ((n_gt < n_sel) & (n_ge >= n_sel))
    thr, n_ge = lax.cond(jnp.min(jnp.where(settled, 1, 0)) > 0,
                         lambda: (thr, n_ge), lambda: kth_largest(lambda c: score_ref[c]))
    thr = jnp.where(select_all, -jnp.inf, thr)

    def to_excess(c, carry):
        s = score_ref[c]
        above = jnp.where(s > thr, jnp.maximum(s - thr, TINY), 0.0)
        score_ref[c] = jnp.where(s >= thr, above, -jnp.inf)
        return carry
    lax.fori_loop(0, n_chunks, to_excess, 0)

    def refine():
        n_above = count(lambda c: score_ref[c] > 0.0)
        crowded = jnp.max(jnp.where((n_above >= n_sel) & jnp.logical_not(select_all), 1, 0)) > 0

        def split():
            t2, _ = kth_largest(lambda c: score_ref[c])
            return t2, n_sel - count(lambda c: score_ref[c] > t2)
        thr2, need = lax.cond(crowded, split, lambda: (jnp.zeros((1, qb), F32), n_sel - n_above))

        def tie_body(it, j):
            trial = j | (jnp.int32(1) << (12 - it))
            g = count(lambda c: (score_ref[c] == thr2) & (c * kc + kio < trial))
            return jnp.where(g <= need, trial, j)
        return thr2, lax.fori_loop(0, 13, tie_body, jnp.zeros((1, qb), I32))

    tied = jnp.max(jnp.where((n_ge > n_sel) & jnp.logical_not(select_all), 1, 0)) > 0
    thr2, j_lim = lax.cond(tied, refine, lambda: (jnp.zeros((1, qb), F32),
                                                  jnp.full((1, qb), 2 ** 30, I32)))
    thr2 = jnp.where(select_all, 0.0, thr2)
    j_lim = jnp.where(select_all, jnp.int32(2 ** 30), j_lim)

    q_att = [pair_rows(qa_ref, p) for p in pairs]
    m_ref[...] = jnp.full_like(m_ref, NEG_BIG)
    l_ref[...] = jnp.zeros_like(l_ref)
    acc_ref[...] = jnp.zeros_like(acc_ref)

    def att_body(c, carry):
        start = pl.multiple_of(c * kc, kc)
        kpos = start + kio
        ex = score_ref[c]
        sel = (kpos < limit) & ((ex > thr2) | ((ex == thr2) & (kpos < j_lim)))
        bias = twice(jnp.where(sel, 0.0, NEG_BIG))
        keys = ka_ref[pl.ds(start, kc), :]
        s = [_nt_dot(keys, q_att[p]) + bias for p in pairs]
        m_old = [m_ref[p] for p in pairs]
        m_new = [jnp.maximum(m_old[p], jnp.max(_slab_reduce(s[p], jnp.maximum), axis=0,
                                               keepdims=True)) for p in pairs]
        alpha = [jnp.exp(m_old[p] - m_new[p]) for p in pairs]
        pe = [jnp.exp(s[p] - m_new[p]) for p in pairs]
        pv = []
        for p in pairs:
            pb = pe[p].astype(BF16)
            tot = None
            for v in range(va_pieces):
                part = jnp.dot(va_ref[va_pieces * c + v], pb[v * ROW_TILE:(v + 1) * ROW_TILE],
                               preferred_element_type=F32)
                tot = part if tot is None else tot + part
            pv.append(tot)
        for p in pairs:
            l_ref[p] = alpha[p] * l_ref[p] + jnp.sum(_slab_reduce(pe[p], jnp.add), axis=0,
                                                     keepdims=True)
            acc_ref[p] = acc_ref[p] * alpha[p] + pv[p]
            m_ref[p] = m_new[p]
        return carry

    lax.fori_loop(0, n_chunks, att_body, 0)
    for p in pairs:
        out = acc_ref[p] / l_ref[p]
        both = jnp.concatenate([out[:, :qb], out[:, qb:]], axis=0)
        o_ref[:, 2 * p * HEAD_DIM:2 * (p + 1) * HEAD_DIM] = both.T.astype(BF16)


def _dsa(qa, qi, wi, ka, ki, va, batch, seq):
    n = batch * seq
    nqb = seq // DSA_QB
    n_sel = min(MAX_SELECTED_KEYS, seq // 4)
    head_spec = pl.BlockSpec((N_HEADS, DSA_QB, HEAD_DIM), lambda b, i: (0, b * nqb + i, 0))
    kv_spec = pl.BlockSpec((seq, HEAD_DIM), lambda b, i: (b, 0))
    pair_stat = pltpu.VMEM((N_HEADS // 2, 1, 2 * DSA_QB), F32)
    return pl.pallas_call(
        functools.partial(_dsa_kernel, n_sel),
        grid=(batch, nqb),
        in_specs=[head_spec, head_spec,
                  pl.BlockSpec((N_HEADS, DSA_QB), lambda b, i: (0, b * nqb + i)),
                  kv_spec, kv_spec,
                  pl.BlockSpec((seq // ROW_TILE, HEAD_DIM, ROW_TILE), lambda b, i: (b, 0, 0))],
        out_specs=pl.BlockSpec((DSA_QB, N_HEADS * HEAD_DIM), lambda b, i: (b * nqb + i, 0)),
        out_shape=jax.ShapeDtypeStruct((n, N_HEADS * HEAD_DIM), BF16),
        scratch_shapes=[pltpu.VMEM((seq // DSA_KC, DSA_KC, DSA_QB), F32),
                        pltpu.VMEM((32, seq // 256, 8, DSA_QB), I32),
                        pltpu.VMEM((seq // 256, 8, DSA_QB), I32), pair_stat, pair_stat,
                        pltpu.VMEM((N_HEADS // 2, HEAD_DIM, 2 * DSA_QB), F32)],
        compiler_params=pltpu.CompilerParams(vmem_limit_bytes=VMEM_LIMIT),
        name="dsa",
    )(qa, qi, wi, ka, ki, va)


def _sb_kernel(q_ref, k_ref, v_ref, o_ref):
    i = pl.program_id(2)
    t = SB_T
    row = lax.broadcasted_iota(I32, (t, t), 0)
    col = lax.broadcasted_iota(I32, (t, t), 1)
    tri = jnp.where(row > col, 1.0, 0.0).astype(BF16)
    causal = col < row

    def block(j, state, diag):
        heads = range(SB_HEADS)
        start = pl.multiple_of(j * t, t)
        zs = [_nt_dot(q_ref[hd], k_ref[hd, pl.ds(start, t), :]) for hd in heads]
        lks = []
        for z in zs:
            lk = -(jnp.maximum(z, 0.0) + jnp.log(1.0 + jnp.exp(-jnp.abs(z))))
            lks.append(jnp.where(causal, lk, 0.0) if diag else lk)
        parts = [_split_bf16(lk) for lk in lks]
        laters = [jnp.dot(hi, tri, preferred_element_type=F32)
                  + jnp.dot(lo, tri, preferred_element_type=F32)
                  for hi, lo in parts]
        out = []
        for hd in heads:
            carry, acc = state[hd]
            a = jnp.exp(zs[hd] + lks[hd] + laters[hd] + carry)
            if diag:
                a = jnp.where(causal, a, 0.0)
            acc = acc + jnp.dot(a.astype(BF16), v_ref[hd, pl.ds(start, t), :],
                                preferred_element_type=F32)
            out.append((carry + jnp.sum(lks[hd], axis=-1, keepdims=True), acc))
        return tuple(out)

    state = tuple((jnp.zeros((t, 1), F32), jnp.zeros((t, LANES), F32))
                  for _ in range(SB_HEADS))
    state = block(i, state, True)

    def live(loop):
        jj, st = loop
        top = st[0][0]
        for hd in range(1, SB_HEADS):
            top = jnp.maximum(top, st[hd][0])
        return (jj < i) & (jnp.max(top) > SB_LOG_ZERO)

    _, state = lax.while_loop(live, lambda loop: (loop[0] + 1, block(i - 1 - loop[0], loop[1], False)),
                              (jnp.int32(0), state))
    for p in range(SB_HEADS // 2):
        pair = state[2 * p][1] + state[2 * p + 1][1]
        o_ref[:, p * LANES:(p + 1) * LANES] = pair.astype(BF16)


def _sb(qb, kb, vb, batch, seq):
    n = batch * seq
    nq = seq // SB_T
    assert SB_HEADS == N_HEADS
    q_spec = pl.BlockSpec((SB_HEADS, SB_T, HEAD_DIM), lambda b, h, i: (h, b * nq + i, 0))
    k_spec = pl.BlockSpec((SB_HEADS, seq, HEAD_DIM), lambda b, h, i: (h, b, 0))
    v_spec = pl.BlockSpec((SB_HEADS, seq, LANES), lambda b, h, i: (h, b, 0))
    return pl.pallas_call(
        _sb_kernel,
        grid=(batch, N_HEADS // SB_HEADS, nq),
        in_specs=[q_spec, k_spec, v_spec],
        out_specs=pl.BlockSpec((SB_T, N_HEADS * HEAD_DIM), lambda b, h, i: (b * nq + i, 0)),
        out_shape=jax.ShapeDtypeStruct((n, N_HEADS * HEAD_DIM), BF16),
        compiler_params=pltpu.CompilerParams(vmem_limit_bytes=VMEM_LIMIT),
        name="stickbreak",
    )(qb, kb, vb)


def _outproj_kernel(x_ref, oa_ref, ob_ref, ga_ref, gb_ref, wpa_ref, wpb_ref, wo_ref,
                    gffn_ref, wr_hi_ref, wr_lo_ref, br_ref,
                    h_ref, hn_ref, idx_ref, gate_ref, rank_ref, cnt_ref, carry_ref):
    step = pl.program_id(0)
    tm = x_ref.shape[0]

    @pl.when(step == 0)
    def _():
        carry_ref[...] = jnp.zeros_like(carry_ref)

    pa = jnp.dot(oa_ref[...], wpa_ref[...], preferred_element_type=F32)
    pb = jnp.dot(ob_ref[...], wpb_ref[...], preferred_element_type=F32)
    merged = jax.nn.sigmoid(ga_ref[...]) * pa + jax.nn.sigmoid(gb_ref[...]) * pb
    h = x_ref[...] + jnp.dot(merged.astype(BF16), wo_ref[...], preferred_element_type=F32)
    h_ref[...] = h
    ms = jnp.mean(h * h, axis=-1, keepdims=True)
    hn = h * lax.rsqrt(ms + RMS_EPS) * gffn_ref[...]
    hn_ref[...] = hn

    hn_hi, hn_lo = _split_bf16(hn)
    logits = (jnp.dot(hn_hi, wr_hi_ref[...], preferred_element_type=F32)
              + jnp.dot(hn_lo, wr_hi_ref[...], preferred_element_type=F32)
              + jnp.dot(hn_hi, wr_lo_ref[...], preferred_element_type=F32)) + br_ref[...]

    lane = lax.broadcasted_iota(I32, (tm, N_EXPERTS), 1)
    lane_k = lax.broadcasted_iota(I32, (tm, TOP_K), 1)
    vals = logits
    picks, top_vals, top_idx = [], [], []
    for _ in range(TOP_K):
        mx = jnp.max(vals, axis=-1, keepdims=True)
        first = jnp.min(jnp.where(vals == mx, lane, N_EXPERTS), axis=-1, keepdims=True)
        pick = lane == first
        picks.append(pick)
        top_vals.append(mx)
        top_idx.append(first)
        vals = jnp.where(pick, -jnp.inf, vals)
    exps = [jnp.exp(v - top_vals[0]) for v in top_vals]
    denom = exps[0] + exps[1] + exps[2] + exps[3]

    any_pick = picks[0] | picks[1] | picks[2] | picks[3]
    onehot = jnp.where(any_pick, 1.0, 0.0)
    r_i = lax.broadcasted_iota(I32, (tm, tm), 0)
    c_i = lax.broadcasted_iota(I32, (tm, tm), 1)
    lower = jnp.where(c_i < r_i, 1.0, 0.0).astype(BF16)
    pos = jnp.dot(lower, onehot.astype(BF16), preferred_element_type=F32) + carry_ref[...]
    carry_ref[...] = carry_ref[...] + jnp.sum(onehot, axis=0, keepdims=True)
    cnt_ref[...] = carry_ref[...].astype(I32)

    idx4 = jnp.zeros((tm, TOP_K), I32)
    gate4 = jnp.zeros((tm, TOP_K), F32)
    rank4 = jnp.zeros((tm, TOP_K), I32)
    for k in range(TOP_K):
        rk = jnp.sum(jnp.where(picks[k], pos, 0.0), axis=-1, keepdims=True).astype(I32)
        idx4 = jnp.where(lane_k == k, top_idx[k], idx4)
        gate4 = jnp.where(lane_k == k, exps[k] / denom, gate4)
        rank4 = jnp.where(lane_k == k, rk, rank4)
    idx_ref[...] = idx4
    gate_ref[...] = gate4
    rank_ref[...] = rank4


def _outproj(x2, oa, ob, ga, gb, w_proj_a, w_proj_b, w_out, norm_ffn_g, w_router, b_router):
    n = x2.shape[0]
    tm = ROW_TILE
    width = N_HEADS * HEAD_DIM
    wpa = w_proj_a.astype(BF16)
    wpb = w_proj_b.astype(BF16)
    wr_hi = w_router.astype(BF16)
    wr_lo = (w_router - wr_hi.astype(F32)).astype(BF16)
    full = lambda shape: pl.BlockSpec(shape, lambda i: (0,) * len(shape))
    row = lambda width: pl.BlockSpec((tm, width), lambda i: (i, 0))
    return pl.pallas_call(
        _outproj_kernel,
        grid=(n // tm,),
        in_specs=[row(D_MODEL), row(width), row(width), row(D_MODEL), row(D_MODEL),
                  full((width, D_MODEL)), full((width, D_MODEL)),
                  full((D_MODEL, D_MODEL)), full((1, D_MODEL)),
                  full((D_MODEL, N_EXPERTS)), full((D_MODEL, N_EXPERTS)), full((1, N_EXPERTS))],
        out_specs=[row(D_MODEL), row(D_MODEL), row(TOP_K), row(TOP_K), row(TOP_K),
                   full((1, N_EXPERTS))],
        out_shape=[jax.ShapeDtypeStruct((n, D_MODEL), F32), jax.ShapeDtypeStruct((n, D_MODEL), F32),
                   jax.ShapeDtypeStruct((n, TOP_K), I32), jax.ShapeDtypeStruct((n, TOP_K), F32),
                   jax.ShapeDtypeStruct((n, TOP_K), I32), jax.ShapeDtypeStruct((1, N_EXPERTS), I32)],
        scratch_shapes=[pltpu.VMEM((1, N_EXPERTS), F32)],
        compiler_params=pltpu.CompilerParams(vmem_limit_bytes=VMEM_LIMIT,
                                             dimension_semantics=("arbitrary",)),
        name="outproj_router",
    )(x2, oa, ob, ga, gb, wpa, wpb, w_out.astype(BF16), norm_ffn_g[None, :],
      wr_hi, wr_lo, b_router[None, :])


def _dispatch_kernel(dest_ref, fill_lo_ref, fill_hi_ref, hn_ref, xs_ref, zero_ref, sem, zsem):
    s = pl.program_id(0)
    dt = hn_ref.shape[0]

    def tok(tt, carry):
        t = s * dt + tt
        for k in range(TOP_K):
            d = dest_ref[t * TOP_K + k]
            pltpu.make_async_copy(hn_ref.at[pl.ds(tt, 1)], xs_ref.at[pl.ds(d, 1)],
                                  sem).start(priority=k % 2)
        return carry
    lax.fori_loop(0, dt, tok, 0, unroll=4)

    @pl.when(s == 0)
    def _():
        zero_ref[...] = jnp.zeros_like(zero_ref)

        def expert(e, total):
            lo, hi = fill_lo_ref[e], fill_hi_ref[e]

            def fill(r, carry):
                pltpu.make_async_copy(zero_ref, xs_ref.at[pl.ds(r, 1)], zsem).start()
                return carry
            lax.fori_loop(lo, hi, fill, 0)
            return total + (hi - lo)
        total = lax.fori_loop(0, fill_lo_ref.shape[0], expert, jnp.int32(0))

        def drain(r, carry):
            pltpu.make_async_copy(zero_ref, xs_ref.at[pl.ds(0, 1)], zsem).wait()
            return carry
        lax.fori_loop(0, total, drain, 0)

    for _ in range(TOP_K):
        pltpu.make_async_copy(hn_ref, xs_ref.at[pl.ds(0, dt)], sem).wait()


def _dispatch(dest_flat, fill_lo, fill_hi, hn, rows):
    n = hn.shape[0]
    dt = DISPATCH_T
    return pl.pallas_call(
        _dispatch_kernel,
        grid_spec=pltpu.PrefetchScalarGridSpec(
            num_scalar_prefetch=3,
            grid=(n // dt,),
            in_specs=[pl.BlockSpec((dt, D_MODEL), lambda i, d, lo, hi: (i, 0))],
            out_specs=pl.BlockSpec(memory_space=pl.ANY),
            scratch_shapes=[pltpu.VMEM((1, D_MODEL), F32),
                            pltpu.SemaphoreType.DMA(()), pltpu.SemaphoreType.DMA(())]),
        out_shape=jax.ShapeDtypeStruct((rows, D_MODEL), F32),
        compiler_params=pltpu.CompilerParams(vmem_limit_bytes=VMEM_LIMIT,
                                             dimension_semantics=("arbitrary",),
                                             has_side_effects=True),
        name="dispatch",
    )(dest_flat, fill_lo, fill_hi, hn)


def _expert_kernel(be_ref, nu_ref, x_ref, wgu_hbm, bgu_ref, wd_hbm, bd_ref, y_ref,
                   wgu_bf, wd_bf, wgu_buf, wd_buf, run_ref, sems):
    b = pl.program_id(0)
    n_used = nu_ref[0]
    prev = be_ref[jnp.maximum(b - 1, 0)]

    def weight_copies(e, slot):
        return (pltpu.make_async_copy(wgu_hbm.at[e], wgu_buf.at[slot], sems.at[0, slot]),
                pltpu.make_async_copy(wd_hbm.at[e], wd_buf.at[slot], sems.at[1, slot]))

    @pl.when(b == 0)
    def _():
        run_ref[0] = 0
        for cp in weight_copies(be_ref[0], 0):
            cp.start()

    @pl.when((b < n_used) & ((b == 0) | (be_ref[b] != prev)))
    def _():
        slot = run_ref[0] % 2
        run_ref[0] = run_ref[0] + 1
        nxt = lax.while_loop(lambda j: (j < n_used) & (be_ref[jnp.minimum(j, n_used - 1)] == be_ref[b]),
                             lambda j: j + 1, b + 1)

        @pl.when(nxt < n_used)
        def _():
            for cp in weight_copies(be_ref[jnp.minimum(nxt, n_used - 1)], 1 - slot):
                cp.start()

        for cp in weight_copies(be_ref[b], slot):
            cp.wait()
        wgu_bf[...] = wgu_buf[slot].astype(BF16)
        wd_bf[...] = wd_buf[slot].astype(BF16)

    @pl.when(b < n_used)
    def _():
        xb = x_ref[...].astype(BF16)
        hgu = jnp.dot(xb, wgu_bf[...], preferred_element_type=F32) + bgu_ref[0]
        gate = jnp.minimum(hgu[:, :D_EXPERT], SWIGLU_LIMIT)
        up = jnp.clip(hgu[:, D_EXPERT:], -SWIGLU_LIMIT, SWIGLU_LIMIT)
        act = (up + 1.0) * gate * jax.nn.sigmoid(SWIGLU_ALPHA * gate)
        y_ref[...] = (jnp.dot(act.astype(BF16), wd_bf[...], preferred_element_type=F32)
                      + bd_ref[0])

    @pl.when(b >= n_used)
    def _():
        y_ref[...] = jnp.zeros_like(y_ref)


def _experts(block_expert, n_used, xs, w_gu, b_gu, w_down, b_down):
    rows = xs.shape[0]
    n_blocks = rows // EXPERT_BLOCK

    def blk(b, be, nu):
        return jnp.minimum(b, nu[0] - 1)

    w_spec = lambda d1, d2: pl.BlockSpec((1, d1, d2), lambda b, be, nu: (be[blk(b, be, nu)], 0, 0))
    return pl.pallas_call(
        _expert_kernel,
        grid_spec=pltpu.PrefetchScalarGridSpec(
            num_scalar_prefetch=2,
            grid=(n_blocks,),
            in_specs=[pl.BlockSpec((EXPERT_BLOCK, D_MODEL), lambda b, be, nu: (blk(b, be, nu), 0)),
                      pl.BlockSpec(memory_space=pl.ANY), w_spec(1, 2 * D_EXPERT),
                      pl.BlockSpec(memory_space=pl.ANY), w_spec(1, D_MODEL)],
            out_specs=pl.BlockSpec((EXPERT_BLOCK, D_MODEL), lambda b, be, nu: (b, 0)),
            scratch_shapes=[pltpu.VMEM((D_MODEL, 2 * D_EXPERT), BF16),
                            pltpu.VMEM((D_EXPERT, D_MODEL), BF16),
                            pltpu.VMEM((2, D_MODEL, 2 * D_EXPERT), F32),
                            pltpu.VMEM((2, D_EXPERT, D_MODEL), F32),
                            pltpu.SMEM((1,), I32),
                            pltpu.SemaphoreType.DMA((2, 2))]),
        out_shape=jax.ShapeDtypeStruct((rows, D_MODEL), F32),
        compiler_params=pltpu.CompilerParams(vmem_limit_bytes=VMEM_LIMIT,
                                             dimension_semantics=("arbitrary",)),
        name="experts",
    )(block_expert, n_used, xs, w_gu, b_gu[:, None, :], w_down, b_down[:, None, :])


def _combine_kernel(dest_ref, h_ref, gate_ref, y_ref, o_ref, buf_ref, sems):
    s = pl.program_id(0)
    n_steps = pl.num_programs(0)
    ct = COMBINE_T

    def issue(step, slot):
        def tok(tt, carry):
            t = step * ct + tt
            for k in range(TOP_K):
                d = dest_ref[t * TOP_K + k]
                pltpu.make_async_copy(y_ref.at[pl.ds(d, 1)], buf_ref.at[slot, k, pl.ds(tt, 1)],
                                      sems.at[slot]).start(priority=k % 2)
            return carry
        lax.fori_loop(0, ct, tok, 0, unroll=4)

    @pl.when(s == 0)
    def _():
        issue(0, 0)

    for parity in range(2):
        @pl.when((s + 1 < n_steps) & ((s + 1) % 2 == parity))
        def _():
            issue(s + 1, parity)

    slot = s % 2
    for k in range(TOP_K):
        pltpu.make_async_copy(y_ref.at[pl.ds(0, ct)], buf_ref.at[slot, k], sems.at[slot]).wait()
    g = gate_ref[...]
    out = h_ref[...]
    for k in range(TOP_K):
        out = out + g[:, k:k + 1] * buf_ref[slot, k]
    o_ref[...] = out


def _combine(dest_flat, h, gates, y):
    n = h.shape[0]
    ct = COMBINE_T
    return pl.pallas_call(
        _combine_kernel,
        grid_spec=pltpu.PrefetchScalarGridSpec(
            num_scalar_prefetch=1,
            grid=(n // ct,),
            in_specs=[pl.BlockSpec((ct, D_MODEL), lambda i, d: (i, 0)),
                      pl.BlockSpec((ct, TOP_K), lambda i, d: (i, 0)),
                      pl.BlockSpec(memory_space=pl.ANY)],
            out_specs=pl.BlockSpec((ct, D_MODEL), lambda i, d: (i, 0)),
            scratch_shapes=[pltpu.VMEM((2, TOP_K, ct, D_MODEL), F32),
                            pltpu.SemaphoreType.DMA((2,))]),
        out_shape=jax.ShapeDtypeStruct((n, D_MODEL), F32),
        compiler_params=pltpu.CompilerParams(vmem_limit_bytes=VMEM_LIMIT,
                                             dimension_semantics=("arbitrary",)),
        name="combine",
    )(dest_flat, h, gates, y)


def _moe(h, hn, top_idx, gates, rank, counts, w_gate_up, b_gate_up, w_down, b_down):
    n = h.shape[0]
    m = n * TOP_K
    n_blocks = (m + N_EXPERTS * (EXPERT_BLOCK - 1) + EXPERT_BLOCK - 1) // EXPERT_BLOCK
    rows = n_blocks * EXPERT_BLOCK
    counts = counts[0]
    padded = ((counts + EXPERT_BLOCK - 1) // EXPERT_BLOCK) * EXPERT_BLOCK
    pend = jnp.cumsum(padded).astype(I32)
    pstart = pend - padded
    dest = (pstart[top_idx] + rank).reshape(m).astype(I32)
    block_start = jnp.arange(n_blocks, dtype=I32) * EXPERT_BLOCK
    block_expert = jnp.minimum(jnp.sum(pend[None, :] <= block_start[:, None], axis=1),
                               N_EXPERTS - 1).astype(I32)
    n_used = (pend[-1:] // EXPERT_BLOCK).astype(I32)
    fill_lo = jnp.concatenate([pstart + counts, pend[-1:]]).astype(I32)
    fill_hi = jnp.concatenate([pend, jnp.full((1,), rows, I32)])
    xs = _dispatch(dest, fill_lo, fill_hi, hn, rows)
    y = _experts(block_expert, n_used, xs, w_gate_up, b_gate_up, w_down, b_down)
    return _combine(dest, h, gates, y)


def kernel(x, norm_mix_g, w_in, q_norm_g, k_norm_g, w_proj_a, w_proj_b, w_out, norm_ffn_g,
           w_router, b_router, w_gate_up, b_gate_up, w_down, b_down):
    batch, seq, d = x.shape
    h = x.reshape(batch * seq, d)
    for l in range(norm_mix_g.shape[0]):
        (qa, qi, qb, kb, vb, ka, ki, va, wi, ga, gb) = _inproj(
            h, norm_mix_g[l], w_in[l], q_norm_g[l], k_norm_g[l], seq)
        oa = _dsa(qa, qi, wi, ka, ki, va, batch, seq)
        ob = _sb(qb, kb, vb, batch, seq)
        h_mid, hn, top_idx, gates, rank, counts = _outproj(
            h, oa, ob, ga, gb, w_proj_a[l], w_proj_b[l], w_out[l], norm_ffn_g[l],
            w_router[l], b_router[l])
        h = _moe(h_mid, hn, top_idx, gates, rank, counts,
                 w_gate_up[l], b_gate_up[l], w_down[l], b_down[l])
    return h.reshape(batch, seq, d)
```
